```python
import math
import jax, jax.numpy as jnp
from jax import lax
import numpy as np

D_MODEL = 1024
BATCH = 32
SEQ = 2048
DEPTH = 4

GRID_W = 64
CTX_LEN = 256

MLA_HEADS = 8
QK_NOPE = 64
QK_ROPE = 32
V_DIM = 64
Q_LORA = 256
KV_LORA = 128
Q_BLOCK = 128
ROPE_BASE = 10000.0
ATTN_SCALE = (QK_NOPE + QK_ROPE) ** -0.5
ATTN_W = MLA_HEADS * V_DIM

POOL_WINDOWS = (2, 4, 8, 16)
POOL_GROUPS = 4
POOL_W = D_MODEL // 2
POOL_GC = POOL_W // POOL_GROUPS

IN_W = Q_LORA + KV_LORA + QK_ROPE + POOL_W
IN_SPLITS = (Q_LORA, Q_LORA + KV_LORA, Q_LORA + KV_LORA + QK_ROPE)
MIX_W = ATTN_W + POOL_W

FNET_GROUPS = 4
FNET_GC = D_MODEL // FNET_GROUPS

D_FF_DENSE = 2816
N_EXPERTS = 8
TOP_K = 2
D_FF_EXPERT = 3584

N_EVEN = (DEPTH + 1) // 2
N_ODD = DEPTH // 2
EPS = 1e-6

kernel_name = "hybrid_mla_pool_fnet_moe_prefix_dit"


def rmsnorm(x, g):
    x32 = x.astype(jnp.float32)
    y = x32 * lax.rsqrt(jnp.mean(x32 * x32, axis=-1, keepdims=True) + EPS)
    return (y * g.astype(jnp.float32)).astype(x.dtype)


def modulate(x, g, shift, scale):
    return rmsnorm(x, g) * (1.0 + scale) + shift


def axial_rope_tables(rows, dtype):
    row = jnp.repeat(jnp.arange(rows), GRID_W).astype(jnp.float32)
    col = jnp.tile(jnp.arange(GRID_W), rows).astype(jnp.float32)
    half = QK_ROPE // 2
    inv = 1.0 / (ROPE_BASE ** (jnp.arange(0, half, 2, dtype=jnp.float32) / half))
    ang_r = row[:, None] * inv[None, :]
    ang_c = col[:, None] * inv[None, :]
    return (jnp.cos(ang_r).astype(dtype), jnp.sin(ang_r).astype(dtype),
            jnp.cos(ang_c).astype(dtype), jnp.sin(ang_c).astype(dtype))


def rotate(x, cos, sin):
    x1, x2 = jnp.split(x, 2, axis=-1)
    return jnp.concatenate([x1 * cos - x2 * sin, x1 * sin + x2 * cos], axis=-1)


def axial_rope(x, tabs):
    cos_r, sin_r, cos_c, sin_c = tabs
    xr, xc = jnp.split(x, 2, axis=-1)
    return jnp.concatenate([rotate(xr, cos_r, sin_r), rotate(xc, cos_c, sin_c)], axis=-1)


def mla_q(c_q, g_cq, w_uq):
    B, L, _ = c_q.shape
    q = (rmsnorm(c_q, g_cq) @ w_uq).reshape(B, L, MLA_HEADS, QK_NOPE + QK_ROPE)
    return q[..., :QK_NOPE], q[..., QK_NOPE:]


def mla_kv(c_kv, g_ckv, w_ukv):
    B, L, _ = c_kv.shape
    kv = (rmsnorm(c_kv, g_ckv) @ w_ukv).reshape(B, L, MLA_HEADS, QK_NOPE + V_DIM)
    return kv[..., :QK_NOPE], kv[..., QK_NOPE:]


def mla_attend(q_nope, q_rope, k_nope, k_rope, v):
    s = (jnp.einsum('bqhd,bkhd->bhqk', q_nope, k_nope)
         + jnp.einsum('bqhr,bkr->bhqk', q_rope, k_rope))
    p = jax.nn.softmax(s.astype(jnp.float32) * ATTN_SCALE, axis=-1).astype(v.dtype)
    return jnp.einsum('bhqk,bkhd->bqhd', p, v)


def mla_blocked(q_nope, q_rope, k_nope, k_rope, v):
    B, L, H, _ = q_nope.shape
    nb = L // Q_BLOCK

    def to_blocks(t):
        return jnp.moveaxis(t.reshape(B, nb, Q_BLOCK, *t.shape[2:]), 1, 0)

    out = lax.map(lambda qs: mla_attend(qs[0], qs[1], k_nope, k_rope, v),
                  (to_blocks(q_nope), to_blocks(q_rope)))
    return jnp.moveaxis(out, 0, 1).reshape(B, L, H * V_DIM)


def multiscale_pool(p, pool_w, pool_scale):
    B, L, _ = p.shape
    pg = p.reshape(B, L, POOL_GROUPS, POOL_GC)
    csum = jnp.cumsum(pg.astype(jnp.float32), axis=1)
    csum = jnp.concatenate([jnp.zeros_like(csum[:, :1]), csum], axis=1)
    t = jnp.arange(L)[:, None]
    w = jnp.array(POOL_WINDOWS)[None, :]
    lo = jnp.clip(t - w // 2, 0, L)
    hi = jnp.clip(t - w // 2 + w, 0, L)
    g = jnp.arange(POOL_GROUPS)[None, :]
    win_sum = csum[:, hi, g, :] - csum[:, lo, g, :]
    mean = win_sum / (hi - lo).astype(jnp.float32)[:, :, None]
    mixed = (mean - pg.astype(jnp.float32)).astype(p.dtype)
    y = jnp.einsum('blgc,gcd->blgd', mixed, pool_w).reshape(B, L, POOL_W)
    return y * pool_scale


def fourier_mix(h, w_out):
    B, L, D = h.shape
    hg = h.astype(jnp.float32).reshape(B, L, FNET_GROUPS, FNET_GC)
    f = jnp.fft.fft2(hg, axes=(1, 3), norm='ortho').real.astype(h.dtype)
    return f.reshape(B, L, D) @ w_out


def swiglu(h, w1, w3, w2):
    return (jax.nn.silu(h @ w1) * (h @ w3)) @ w2


def moe_swiglu(h, router_w, w1, w3, w2):
    logits = (h @ router_w).astype(jnp.float32)
    top_v, top_i = lax.top_k(logits, TOP_K)
    top_p = jax.nn.softmax(top_v, axis=-1)
    gates = jnp.sum(jax.nn.one_hot(top_i, N_EXPERTS, dtype=jnp.float32) * top_p[..., None],
                    axis=-2).astype(h.dtype)
    y = jnp.zeros_like(h)
    for e in range(N_EXPERTS):
        y = y + gates[..., e:e + 1] * swiglu(h, w1[e], w3[e], w2[e])
    return y


def setup_inputs(seed: int = 0) -> dict:
    key = jax.random.key(seed)
    ks = iter(jax.random.split(key, 32))
    D = D_MODEL
    HQ = MLA_HEADS * (QK_NOPE + QK_ROPE)
    HKV = MLA_HEADS * (QK_NOPE + V_DIM)

    def nrm(shape, scale):
        return jax.random.normal(next(ks), shape, jnp.float32) * scale

    return {
        "x": nrm((BATCH, SEQ, D), 1.0),
        "c": nrm((BATCH, D), 1.0),
        "ctx": nrm((BATCH, CTX_LEN, D), 1.0),
        "c_ctx": nrm((D,), 1.0),
        "ada_w": nrm((DEPTH, D, 6 * D), 0.5 * D ** -0.5),
        "ada_b": nrm((DEPTH, 6 * D), 0.02),
        "norm_mix_g": 1.0 + nrm((DEPTH, D), 0.02),
        "norm_ffn_g": 1.0 + nrm((DEPTH, D), 0.02),
        "mix_in_w": nrm((N_EVEN, D, IN_W), D ** -0.5),
        "cq_norm_g": 1.0 + nrm((N_EVEN, Q_LORA), 0.02),
        "ckv_norm_g": 1.0 + nrm((N_EVEN, KV_LORA), 0.02),
        "w_uq": nrm((N_EVEN, Q_LORA, HQ), Q_LORA ** -0.5),
        "w_ukv": nrm((N_EVEN, KV_LORA, HKV), KV_LORA ** -0.5),
        "pool_w": nrm((N_EVEN, POOL_GROUPS, POOL_GC, POOL_GC), POOL_GC ** -0.5),
        "pool_scale": 1.0 + nrm((N_EVEN, POOL_W), 0.02),
        "mix_out_w": nrm((N_EVEN, MIX_W, D), MIX_W ** -0.5),
        "ffn_w1": nrm((N_EVEN, D, D_FF_DENSE), D ** -0.5),
        "ffn_w3": nrm((N_EVEN, D, D_FF_DENSE), D ** -0.5),
        "ffn_w2": nrm((N_EVEN, D_FF_DENSE, D), D_FF_DENSE ** -0.5),
        "fnet_out_w": nrm((N_ODD, D, D), D ** -0.5),
        "router_w": nrm((N_ODD, D, N_EXPERTS), D ** -0.5),
        "moe_w1": nrm((N_ODD, N_EXPERTS, D, D_FF_EXPERT), D ** -0.5),
        "moe_w3": nrm((N_ODD, N_EXPERTS, D, D_FF_EXPERT), D ** -0.5),
        "moe_w2": nrm((N_ODD, N_EXPERTS, D_FF_EXPERT, D), D_FF_EXPERT ** -0.5),
        "final_g": 1.0 + nrm((D,), 0.02),
    }


def reference(x, c, ctx, c_ctx, ada_w, ada_b, norm_mix_g, norm_ffn_g, mix_in_w, cq_norm_g,
              ckv_norm_g, w_uq, w_ukv, pool_w, pool_scale, mix_out_w, ffn_w1, ffn_w3, ffn_w2,
              fnet_out_w, router_w, moe_w1, moe_w3, moe_w2, final_g):
    B, L, D = x.shape
    Lc = ctx.shape[1]
    ROWS = L // GRID_W
    tabs = axial_rope_tables(ROWS, x.dtype)
    tabs_q = tuple(t[:, None, :] for t in tabs)
    s_lat = jax.nn.silu(c)
    s_ctx = jax.nn.silu(c_ctx)[None, :]

    for i in range(DEPTH):
        sh1, sc1, g1, sh2, sc2, g2 = jnp.split((s_lat @ ada_w[i] + ada_b[i])[:, None, :], 6, axis=-1)
        cs1, cc1, cg1, cs2, cc2, cg2 = jnp.split((s_ctx @ ada_w[i] + ada_b[i])[:, None, :], 6, axis=-1)
        ctx_needed = any(j % 2 == 0 for j in range(i + 1, DEPTH))

        h = modulate(x, norm_mix_g[i], sh1, sc1)
        if i % 2 == 0:
            e = i // 2
            hc = modulate(ctx, norm_mix_g[i], cs1, cc1)
            cq_c, ckv_c, kr_c, pool_c = jnp.split(hc @ mix_in_w[e], IN_SPLITS, axis=-1)
            kn_c, v_c = mla_kv(ckv_c, ckv_norm_g[e], w_ukv[e])
            cq_l, ckv_l, kr_l, pool_l = jnp.split(h @ mix_in_w[e], IN_SPLITS, axis=-1)
            qn_l, qr_l = mla_q(cq_l, cq_norm_g[e], w_uq[e])
            kn_l, v_l = mla_kv(ckv_l, ckv_norm_g[e], w_ukv[e])
            qr_l = axial_rope(qr_l, tabs_q)
            kr_l = axial_rope(kr_l, tabs)
            attn_l = mla_blocked(qn_l, qr_l,
                                 jnp.concatenate([kn_l, kn_c], axis=1),
                                 jnp.concatenate([kr_l, kr_c], axis=1),
                                 jnp.concatenate([v_l, v_c], axis=1))
            pool_out_l = multiscale_pool(pool_l, pool_w[e], pool_scale[e])
            mix_l = jnp.concatenate([attn_l, pool_out_l], axis=-1) @ mix_out_w[e]
            if ctx_needed:
                qn_c, qr_c = mla_q(cq_c, cq_norm_g[e], w_uq[e])
                attn_c = mla_attend(qn_c, qr_c, kn_c, kr_c, v_c).reshape(B, Lc, ATTN_W)
                pool_out_c = multiscale_pool(pool_c, pool_w[e], pool_scale[e])
                mix_c = jnp.concatenate([attn_c, pool_out_c], axis=-1) @ mix_out_w[e]
        else:
            o = i // 2
            mix_l = fourier_mix(h, fnet_out_w[o])
            if ctx_needed:
                hc = modulate(ctx, norm_mix_g[i], cs1, cc1)
                mix_c = fourier_mix(hc, fnet_out_w[o])

        x = x + g1 * mix_l
        h = modulate(x, norm_ffn_g[i], sh2, sc2)
        if i % 2 == 0:
            x = x + g2 * swiglu(h, ffn_w1[i // 2], ffn_w3[i // 2], ffn_w2[i // 2])
        else:
            x = x + g2 * moe_swiglu(h, router_w[i // 2], moe_w1[i // 2], moe_w3[i // 2], moe_w2[i // 2])

        if ctx_needed:
            ctx = ctx + cg1 * mix_c
            hc = modulate(ctx, norm_ffn_g[i], cs2, cc2)
            if i % 2 == 0:
                ctx = ctx + cg2 * swiglu(hc, ffn_w1[i // 2], ffn_w3[i // 2], ffn_w2[i // 2])
            else:
                ctx = ctx + cg2 * moe_swiglu(hc, router_w[i // 2], moe_w1[i // 2], moe_w3[i // 2], moe_w2[i // 2])

    return rmsnorm(x, final_g)
```

```python
import functools
import math

import numpy as np
import jax
import jax.numpy as jnp
from jax import lax
from jax.experimental import pallas as pl
from jax.experimental.pallas import tpu as pltpu

F32 = jnp.float32
BF16 = jnp.bfloat16

DEPTH = 4
GRID_W = 64
MLA_HEADS = 8
QK_NOPE = 64
QK_ROPE = 32
V_DIM = 64
Q_LORA = 256
KV_LORA = 128
ROPE_BASE = 10000.0
ATTN_SCALE = (QK_NOPE + QK_ROPE) ** -0.5
HEAD_PAD = 128
POOL_WINDOWS = (2, 4, 8, 16)
POOL_GC = 128
FNET_GROUPS = 4
N_EXPERTS = 8
EPS = 1e-6

VMEM_LIMIT = 56 * 1024 * 1024


def _cp(*sem):
    return pltpu.CompilerParams(dimension_semantics=sem, vmem_limit_bytes=VMEM_LIMIT)


def _tile(n, pref):
    t = min(n, pref)
    while n % t:
        t //= 2
    assert t % 8 == 0, (n, pref)
    return t


def _rms(x, g):
    return x * lax.rsqrt(jnp.mean(x * x, axis=-1, keepdims=True) + EPS) * g


def _modulate(x, g, sh, sc):
    return _rms(x, g) * (1.0 + sc) + sh


def _silu(a):
    return a / (1.0 + jnp.exp(-a))


def _dot(a, b):
    return jnp.dot(a, b, preferred_element_type=F32)


def _adaln_kernel(c_ref, w_ref, b_ref, o_ref):
    s = _silu(c_ref[...])
    o_ref[...] = jnp.dot(s, w_ref[...], precision=lax.Precision.HIGHEST,
                         preferred_element_type=F32) + b_ref[...]


def _adaln(cvec, ada_w, ada_b):
    depth, d, n6 = ada_w.shape
    r = cvec.shape[0]
    tn = _tile(n6, 1536)
    return pl.pallas_call(
        _adaln_kernel,
        grid=(depth, n6 // tn),
        in_specs=[pl.BlockSpec((r, d), lambda l, j: (0, 0)),
                  pl.BlockSpec((None, d, tn), lambda l, j: (l, 0, j)),
                  pl.BlockSpec((None, 1, tn), lambda l, j: (l, 0, j))],
        out_specs=pl.BlockSpec((None, r, tn), lambda l, j: (l, 0, j)),
        out_shape=jax.ShapeDtypeStruct((depth, r, n6), F32),
        compiler_params=_cp("arbitrary", "arbitrary"),
        name="adaln",
    )(cvec, ada_w, ada_b.reshape(depth, 1, n6))


def _inproj_kernel(x_ref, sh_ref, sc_ref, g_ref, win_ref, gcq_ref, gckv_ref, wq_ref, wkv_ref,
                   tab_ref, q_ref, k_ref, v_ref, p_ref):
    hw = MLA_HEADS * HEAD_PAD
    h = _modulate(x_ref[...], g_ref[...], sh_ref[...], sc_ref[...]).astype(BF16)
    y = _dot(h, win_ref[...])
    cqn = _rms(y[:, 0:Q_LORA], gcq_ref[...]).astype(BF16)
    ckvn = _rms(y[:, Q_LORA:Q_LORA + KV_LORA], gckv_ref[...]).astype(BF16)
    qab = _dot(cqn, wq_ref[...])
    kv = _dot(ckvn, wkv_ref[...])
    tab = tab_ref[...]
    cos_q, sin_q, cos_k, sin_k = (tab[:, i * HEAD_PAD:(i + 1) * HEAD_PAD] for i in range(4))
    kr = y[:, 896:1024] * cos_k + y[:, 1024:1152] * sin_k
    for hd in range(MLA_HEADS):
        lo, hi = hd * HEAD_PAD, (hd + 1) * HEAD_PAD
        q_ref[:, lo:hi] = (qab[:, lo:hi] * cos_q + qab[:, hw + lo:hw + hi] * sin_q).astype(BF16)
        k_ref[:, lo:hi] = (kv[:, lo:hi] + kr).astype(BF16)
    v_ref[...] = kv[:, hw:hw + MLA_HEADS * V_DIM].astype(BF16)
    p_ref[...] = y[:, 384:896]


def _inproj(xf, sh, sc, g, wts, tab, seq):
    n, d = xf.shape
    tm = _tile(seq, 512)
    tpb = seq // tm
    win, gcq, gckv, wq, wkv = wts
    hw = MLA_HEADS * HEAD_PAD
    vw = MLA_HEADS * V_DIM
    row = lambda i: (i, 0)
    per_b = lambda i: (i // tpb, 0, 0)
    full = lambda i: (0, 0)
    return pl.pallas_call(
        _inproj_kernel,
        grid=(n // tm,),
        in_specs=[pl.BlockSpec((tm, d), row),
                  pl.BlockSpec((None, 1, d), per_b),
                  pl.BlockSpec((None, 1, d), per_b),
                  pl.BlockSpec((1, d), full),
                  pl.BlockSpec(win.shape, full),
                  pl.BlockSpec(gcq.shape, full),
                  pl.BlockSpec(gckv.shape, full),
                  pl.BlockSpec(wq.shape, full),
                  pl.BlockSpec(wkv.shape, full),
                  pl.BlockSpec((tm, 4 * HEAD_PAD), lambda i: (i % tpb, 0))],
        out_specs=[pl.BlockSpec((tm, hw), row), pl.BlockSpec((tm, hw), row),
                   pl.BlockSpec((tm, vw), row), pl.BlockSpec((tm, 512), row)],
        out_shape=[jax.ShapeDtypeStruct((n, hw), BF16), jax.ShapeDtypeStruct((n, hw), BF16),
                   jax.ShapeDtypeStruct((n, vw), BF16), jax.ShapeDtypeStruct((n, 512), F32)],
        compiler_params=_cp("arbitrary"),
        name="inproj",
    )(xf, sh, sc, g, win, gcq, gckv, wq, wkv, tab)


def _attn_kernel(*refs, n_chunks):
    q_ref = refs[0]
    k_refs = refs[1:1 + n_chunks]
    v_refs = refs[1 + n_chunks:1 + 2 * n_chunks]
    o_ref = refs[1 + 2 * n_chunks]
    nt = (((1,), (1,)), ((), ()))
    outs = []
    for hh in range(2):
        lo, hi = hh * HEAD_PAD, (hh + 1) * HEAD_PAD
        qh = q_ref[:, lo:hi]
        s = [lax.dot_general(qh, k[:, lo:hi], nt, preferred_element_type=F32) for k in k_refs]
        m = functools.reduce(jnp.maximum, [jnp.max(t, axis=-1, keepdims=True) for t in s])
        p = [jnp.exp(t - m) for t in s]
        l = functools.reduce(jnp.add, [jnp.sum(t, axis=-1, keepdims=True) for t in p])
        o = functools.reduce(jnp.add, [_dot(t.astype(BF16), v[...]) for t, v in zip(p, v_refs)])
        outs.append(o / l)
    lane = lax.broadcasted_iota(jnp.int32, outs[0].shape, 1)
    o_ref[...] = jnp.where(lane < V_DIM, outs[0], outs[1]).astype(o_ref.dtype)


def _attention(q, ks, vs, batch, seq_q, seq_ks):
    n = q.shape[0]
    tq = _tile(seq_q, 512)
    nq = seq_q // tq
    n_chunks = len(ks)
    hp = MLA_HEADS // 2
    in_specs = [pl.BlockSpec((tq, 2 * HEAD_PAD), lambda b, h, i: (b * nq + i, h))]
    in_specs += [pl.BlockSpec((s, 2 * HEAD_PAD), lambda b, h, i: (b, h)) for s in seq_ks]
    in_specs += [pl.BlockSpec((s, 2 * V_DIM), lambda b, h, i: (b, h)) for s in seq_ks]
    return pl.pallas_call(
        functools.partial(_attn_kernel, n_chunks=n_chunks),
        grid=(batch, hp, nq),
        in_specs=in_specs,
        out_specs=pl.BlockSpec((tq, 2 * V_DIM), lambda b, h, i: (b * nq + i, h)),
        out_shape=jax.ShapeDtypeStruct((n, MLA_HEADS * V_DIM), BF16),
        compiler_params=_cp("arbitrary", "arbitrary", "arbitrary"),
        name="attention",
    )(q, *ks, *vs)


def _pool_kernel(p_ref, w_ref, s_ref, o_ref, *, seq):
    row = lax.broadcasted_iota(jnp.int32, (seq, POOL_GC), 0)

    def down(a, s):
        return jnp.where(row >= s, pltpu.roll(a, s, axis=0), 0.0)

    def up(a, s):
        return jnp.where(row < seq - s, pltpu.roll(a, seq - s, axis=0), 0.0)

    for gi, w in enumerate(POOL_WINDOWS):
        lo, hi = gi * POOL_GC, (gi + 1) * POOL_GC
        half = w // 2
        a = p_ref[:, lo:hi]
        d = a
        u = a
        s = 1
        while s < half:
            d = d + down(d, s)
            u = u + up(u, s)
            s *= 2
        win = down(d, 1) + u
        cnt = jnp.minimum(row + half, seq) - jnp.maximum(row - half, 0)
        mixed = (win / cnt.astype(F32) - a).astype(BF16)
        y = _dot(mixed, w_ref[gi]) * s_ref[:, lo:hi]
        o_ref[:, lo:hi] = y.astype(o_ref.dtype)


def _pool(p, pool_w, pool_scale, batch, seq):
    n, pw = p.shape
    return pl.pallas_call(
        functools.partial(_pool_kernel, seq=seq),
        grid=(batch,),
        in_specs=[pl.BlockSpec((seq, pw), lambda b: (b, 0)),
                  pl.BlockSpec(pool_w.shape, lambda b: (0, 0, 0)),
                  pl.BlockSpec((1, pw), lambda b: (0, 0))],
        out_specs=pl.BlockSpec((seq, pw), lambda b: (b, 0)),
        out_shape=jax.ShapeDtypeStruct((n, pw), BF16),
        compiler_params=_cp("arbitrary"),
        name="pool",
    )(p, pool_w, pool_scale)


def _resid_mm_kernel(*refs, n_in):
    x_ref, g_ref = refs[0], refs[1]
    a_refs = refs[2:2 + n_in]
    w_refs = refs[2 + n_in:2 + 2 * n_in]
    o_ref = refs[2 + 2 * n_in]
    acc = functools.reduce(jnp.add, [_dot(a[...], w[...]) for a, w in zip(a_refs, w_refs)])
    o_ref[...] = x_ref[...] + g_ref[...] * acc


def _resid_mm(xf, gate, acts, ws, seq):
    n, d = xf.shape
    tm = _tile(seq, 1024)
    tpb = seq // tm
    row = lambda i: (i, 0)
    in_specs = [pl.BlockSpec((tm, d), row), pl.BlockSpec((None, 1, d), lambda i: (i // tpb, 0, 0))]
    in_specs += [pl.BlockSpec((tm, a.shape[1]), row) for a in acts]
    in_specs += [pl.BlockSpec(w.shape, lambda i: (0, 0)) for w in ws]
    return pl.pallas_call(
        functools.partial(_resid_mm_kernel, n_in=len(acts)),
        grid=(n // tm,),
        in_specs=in_specs,
        out_specs=pl.BlockSpec((tm, d), row),
        out_shape=jax.ShapeDtypeStruct((n, d), F32),
        compiler_params=_cp("arbitrary"),
        name="resid_mm",
    )(xf, gate, *acts, *ws)


def _ffn_dense_kernel(x_ref, sh_ref, sc_ref, g_ref, gate_ref, w1_ref, w3_ref, w2_ref, o_ref,
                      h_s, acc_s):
    j = pl.program_id(1)

    @pl.when(j == 0)
    def _():
        h_s[...] = _modulate(x_ref[...], g_ref[...], sh_ref[...], sc_ref[...]).astype(BF16)
        acc_s[...] = jnp.zeros_like(acc_s)

    h = h_s[...]
    act = _silu(_dot(h, w1_ref[...])) * _dot(h, w3_ref[...])
    acc_s[...] += _dot(act.astype(BF16), w2_ref[...])

    @pl.when(j == pl.num_programs(1) - 1)
    def _():
        o_ref[...] = x_ref[...] + gate_ref[...] * acc_s[...]


def _ffn_dense(xf, sh, sc, g, gate, w1, w3, w2, seq):
    n, d = xf.shape
    f = w1.shape[1]
    tm = _tile(seq, 1024)
    tpb = seq // tm
    tf = 256
    assert f % tf == 0
    row = lambda i, j: (i, 0)
    per_b = lambda i, j: (i // tpb, 0, 0)
    return pl.pallas_call(
        _ffn_dense_kernel,
        grid=(n // tm, f // tf),
        in_specs=[pl.BlockSpec((tm, d), row),
                  pl.BlockSpec((None, 1, d), per_b),
                  pl.BlockSpec((None, 1, d), per_b),
                  pl.BlockSpec((1, d), lambda i, j: (0, 0)),
                  pl.BlockSpec((None, 1, d), per_b),
                  pl.BlockSpec((d, tf), lambda i, j: (0, j)),
                  pl.BlockSpec((d, tf), lambda i, j: (0, j)),
                  pl.BlockSpec((tf, d), lambda i, j: (j, 0))],
        out_specs=pl.BlockSpec((tm, d), row),
        out_shape=jax.ShapeDtypeStruct((n, d), F32),
        scratch_shapes=[pltpu.VMEM((tm, d), BF16), pltpu.VMEM((tm, d), F32)],
        compiler_params=_cp("arbitrary", "arbitrary"),
        name="ffn_dense",
    )(xf, sh, sc, g, gate, w1, w3, w2)


def _fnet1_kernel(x_ref, sh_ref, sc_ref, g_ref, cs_ref, yc_ref, ys_ref):
    h = _modulate(x_ref[...], g_ref[...], sh_ref[...], sc_ref[...]).astype(BF16)
    gc = cs_ref.shape[0]
    for gi in range(FNET_GROUPS):
        lo, hi = gi * gc, (gi + 1) * gc
        y = _dot(h[:, lo:hi], cs_ref[...])
        yc_ref[:, lo:hi] = y[:, :gc].astype(BF16)
        ys_ref[:, lo:hi] = y[:, gc:].astype(BF16)


def _fnet1(xf, sh, sc, g, cs, seq):
    n, d = xf.shape
    tm = _tile(seq, 1024)
    tpb = seq // tm
    row = lambda i: (i, 0)
    per_b = lambda i: (i // tpb, 0, 0)
    return pl.pallas_call(
        _fnet1_kernel,
        grid=(n // tm,),
        in_specs=[pl.BlockSpec((tm, d), row),
                  pl.BlockSpec((None, 1, d), per_b),
                  pl.BlockSpec((None, 1, d), per_b),
                  pl.BlockSpec((1, d), lambda i: (0, 0)),
                  pl.BlockSpec(cs.shape, lambda i: (0, 0))],
        out_specs=[pl.BlockSpec((tm, d), row), pl.BlockSpec((tm, d), row)],
        out_shape=[jax.ShapeDtypeStruct((n, d), BF16), jax.ShapeDtypeStruct((n, d), BF16)],
        compiler_params=_cp("arbitrary"),
        name="fnet_channels",
    )(xf, sh, sc, g, cs)


def _fnet2_kernel(cl_ref, nsl_ref, yc_ref, ys_ref, x_ref, gate_ref, w_ref, o_ref, *, scale):
    re = _dot(cl_ref[...], yc_ref[...]) + _dot(nsl_ref[...], ys_ref[...])
    f = (re * scale).astype(BF16)
    o_ref[...] = x_ref[...] + gate_ref[...] * _dot(f, w_ref[...])


def _fnet2(cl, nsl, yc, ys, xf, gate, w, batch, seq):
    n, d = xf.shape
    tm = _tile(seq, 512)
    nt = seq // tm
    scale = 1.0 / math.sqrt(seq * (d // FNET_GROUPS))
    return pl.pallas_call(
        functools.partial(_fnet2_kernel, scale=scale),
        grid=(batch, nt),
        in_specs=[pl.BlockSpec((tm, seq), lambda b, i: (i, 0)),
                  pl.BlockSpec((tm, seq), lambda b, i: (i, 0)),
                  pl.BlockSpec((seq, d), lambda b, i: (b, 0)),
                  pl.BlockSpec((seq, d), lambda b, i: (b, 0)),
                  pl.BlockSpec((tm, d), lambda b, i: (b * nt + i, 0)),
                  pl.BlockSpec((None, 1, d), lambda b, i: (b, 0, 0)),
                  pl.BlockSpec(w.shape, lambda b, i: (0, 0))],
        out_specs=pl.BlockSpec((tm, d), lambda b, i: (b * nt + i, 0)),
        out_shape=jax.ShapeDtypeStruct((n, d), F32),
        compiler_params=_cp("arbitrary", "arbitrary"),
        name="fnet_positions",
    )(cl, nsl, yc, ys, xf, gate, w)


def _dft_tables(n):
    k = np.arange(n, dtype=np.int64)
    ang = 2.0 * np.pi * ((k[:, None] * k[None, :]) % n).astype(np.float64) / n
    return np.cos(ang).astype(np.float32), np.sin(ang).astype(np.float32)


def _router_kernel(x_ref, sh_ref, sc_ref, g_ref, rw_ref, h_ref, r_ref):
    h = _modulate(x_ref[...], g_ref[...], sh_ref[...], sc_ref[...])
    h_ref[...] = h
    lg = jnp.dot(h, rw_ref[...], precision=lax.Precision.HIGHEST, preferred_element_type=F32)
    lane = lax.broadcasted_iota(jnp.int32, lg.shape, 1).astype(F32)
    neg = jnp.float32(-jnp.inf)
    lg = jnp.where(lane < N_EXPERTS, lg, neg)
    m1 = jnp.max(lg, axis=-1, keepdims=True)
    i1 = jnp.min(jnp.where(lg == m1, lane, 128.0), axis=-1, keepdims=True)
    lg2 = jnp.where(lane == i1, neg, lg)
    m2 = jnp.max(lg2, axis=-1, keepdims=True)
    i2 = jnp.min(jnp.where(lg2 == m2, lane, 128.0), axis=-1, keepdims=True)
    e = jnp.exp(m2 - m1)
    p1 = 1.0 / (1.0 + e)
    p2 = e / (1.0 + e)
    r_ref[...] = jnp.where(lane == 0, i1, jnp.where(lane == 1, i2, jnp.where(
        lane == 2, p1, jnp.where(lane == 3, p2, 0.0))))


def _router(xf, sh, sc, g, rw, seq):
    n, d = xf.shape
    tm = _tile(seq, 512)
    tpb = seq // tm
    row = lambda i: (i, 0)
    per_b = lambda i: (i // tpb, 0, 0)
    return pl.pallas_call(
        _router_kernel,
        grid=(n // tm,),
        in_specs=[pl.BlockSpec((tm, d), row),
                  pl.BlockSpec((None, 1, d), per_b),
                  pl.BlockSpec((None, 1, d), per_b),
                  pl.BlockSpec((1, d), lambda i: (0, 0)),
                  pl.BlockSpec(rw.shape, lambda i: (0, 0))],
        out_specs=[pl.BlockSpec((tm, d), row), pl.BlockSpec((tm, 128), row)],
        out_shape=[jax.ShapeDtypeStruct((n, d), F32), jax.ShapeDtypeStruct((n, 128), F32)],
        compiler_params=_cp("arbitrary"),
        name="router",
    )(xf, sh, sc, g, rw)


def _route_plan(route, tm):
    n = route.shape[0]
    ids = route[:, 0:2].astype(jnp.int32).reshape(-1)
    probs = route[:, 2:4].reshape(-1)
    oh = (ids[:, None] == jnp.arange(N_EXPERTS, dtype=jnp.int32)[None, :]).astype(jnp.int32)
    cs = jnp.cumsum(oh, axis=0)
    counts = cs[-1]
    padded = ((counts + tm - 1) // tm) * tm
    ends = jnp.cumsum(padded)
    starts = ends - padded
    dest = jnp.sum(oh * (starts[None, :] + cs - 1), axis=1)
    rows = 2 * n + N_EXPERTS * tm
    src = jnp.zeros((rows,), jnp.int32).at[dest].set(jnp.arange(2 * n, dtype=jnp.int32) // 2)
    rp = jnp.zeros((rows,), F32).at[dest].set(probs)
    tiles = rows // tm
    tile_start = jnp.arange(tiles, dtype=jnp.int32) * tm
    te = jnp.minimum(jnp.sum((tile_start[:, None] >= ends[None, :]).astype(jnp.int32), axis=1),
                     N_EXPERTS - 1).astype(jnp.int32)
    n_active = (ends[-1] // tm).astype(jnp.int32).reshape(1)
    return src, rp.reshape(rows, 1), dest, te, n_active


def _gather_kernel(idx_ref, src_hbm, dst_hbm, idx_s, sem_i, sem, *, chunk):
    i = pl.program_id(0)
    cp = pltpu.make_async_copy(idx_ref.at[i], idx_s, sem_i)
    cp.start()
    cp.wait()
    base = i * chunk

    def issue(r, carry):
        t = idx_s[r]
        pltpu.make_async_copy(src_hbm.at[pl.ds(t, 1)], dst_hbm.at[pl.ds(base + r, 1)], sem).start()
        return carry

    lax.fori_loop(0, chunk, issue, 0)

    def drain(r, carry):
        pltpu.make_async_copy(src_hbm.at[pl.ds(0, 1)], dst_hbm.at[pl.ds(base + r, 1)], sem).wait()
        return carry

    lax.fori_loop(0, chunk, drain, 0)


def _gather_rows(src_idx, h, chunk):
    rows = src_idx.shape[0]
    d = h.shape[1]
    assert rows % chunk == 0
    return pl.pallas_call(
        functools.partial(_gather_kernel, chunk=chunk),
        grid=(rows // chunk,),
        in_specs=[pl.BlockSpec((rows // chunk, chunk), lambda i: (0, 0)),
                  pl.BlockSpec(memory_space=pl.ANY)],
        out_specs=pl.BlockSpec(memory_space=pl.ANY),
        out_shape=jax.ShapeDtypeStruct((rows, d), h.dtype),
        scratch_shapes=[pltpu.SMEM((chunk,), jnp.int32), pltpu.SemaphoreType.DMA(()),
                        pltpu.SemaphoreType.DMA(())],
        compiler_params=_cp("arbitrary"),
        name="gather_rows",
    )(src_idx.reshape(rows // chunk, chunk), h)


def _ffn_moe_kernel(te_ref, na_ref, hs_ref, rp_ref, w1_ref, w3_ref, w2_ref, o_ref, h_s, acc_s):
    i = pl.program_id(0)
    j = pl.program_id(1)
    last = pl.num_programs(1) - 1
    active = i < na_ref[0]

    @pl.when(active)
    def _():
        @pl.when(j == 0)
        def _():
            h_s[...] = hs_ref[...].astype(BF16)
            acc_s[...] = jnp.zeros_like(acc_s)

        h = h_s[...]
        act = _silu(_dot(h, w1_ref[...])) * _dot(h, w3_ref[...])
        acc_s[...] += _dot(act.astype(BF16), w2_ref[...])

        @pl.when(j == last)
        def _():
            o_ref[...] = acc_s[...] * rp_ref[...]

    @pl.when(jnp.logical_and(jnp.logical_not(active), j == last))
    def _():
        o_ref[...] = jnp.zeros_like(o_ref)


def _ffn_moe(hs, rp, te, n_active, w1, w3, w2, tm):
    rows, d = hs.shape
    f = w1.shape[2]
    tf = 512
    assert f % tf == 0 and rows % tm == 0
    nj = f // tf

    def jj(i, j, na):
        return jnp.where(i < na[0], j, nj - 1)

    grid_spec = pltpu.PrefetchScalarGridSpec(
        num_scalar_prefetch=2,
        grid=(rows // tm, nj),
        in_specs=[pl.BlockSpec((tm, d), lambda i, j, te, na: (i, 0)),
                  pl.BlockSpec((tm, 1), lambda i, j, te, na: (i, 0)),
                  pl.BlockSpec((None, d, tf), lambda i, j, te, na: (te[i], 0, jj(i, j, na))),
                  pl.BlockSpec((None, d, tf), lambda i, j, te, na: (te[i], 0, jj(i, j, na))),
                  pl.BlockSpec((None, tf, d), lambda i, j, te, na: (te[i], jj(i, j, na), 0))],
        out_specs=pl.BlockSpec((tm, d), lambda i, j, te, na: (i, 0)),
        scratch_shapes=[pltpu.VMEM((tm, d), BF16), pltpu.VMEM((tm, d), F32)],
    )
    return pl.pallas_call(
        _ffn_moe_kernel,
        grid_spec=grid_spec,
        out_shape=jax.ShapeDtypeStruct((rows, d), F32),
        compiler_params=_cp("arbitrary", "arbitrary"),
        name="ffn_moe",
    )(te, n_active, hs, rp, w1, w3, w2)


def _combine_kernel(*refs, tt, final):
    if final:
        pos_ref, o_hbm, x_ref, gate_ref, fg_ref, out_ref, pos_s, buf, sem_i, sem = refs
    else:
        pos_ref, o_hbm, x_ref, gate_ref, out_ref, pos_s, buf, sem_i, sem = refs
    i = pl.program_id(0)
    cp = pltpu.make_async_copy(pos_ref.at[i], pos_s, sem_i)
    cp.start()
    cp.wait()

    def issue(r, carry):
        for c in range(2):
            p = pos_s[2 * r + c]
            pltpu.make_async_copy(o_hbm.at[pl.ds(p, 1)], buf.at[c, pl.ds(r, 1)], sem).start()
        return carry

    lax.fori_loop(0, tt, issue, 0)

    def drain(r, carry):
        for c in range(2):
            pltpu.make_async_copy(o_hbm.at[pl.ds(0, 1)], buf.at[c, pl.ds(r, 1)], sem).wait()
        return carry

    lax.fori_loop(0, tt, drain, 0)
    y = x_ref[...] + gate_ref[...] * (buf[0] + buf[1])
    if final:
        y = _rms(y, fg_ref[...])
    out_ref[...] = y


def _combine(pos, o, xf, gate, seq, final_g=None):
    n, d = xf.shape
    tt = _tile(seq, 512)
    tpb = seq // tt
    final = final_g is not None
    row = lambda i: (i, 0)
    in_specs = [pl.BlockSpec((n // tt, 2 * tt), lambda i: (0, 0)),
                pl.BlockSpec(memory_space=pl.ANY),
                pl.BlockSpec((tt, d), row),
                pl.BlockSpec((None, 1, d), lambda i: (i // tpb, 0, 0))]
    args = [pos.reshape(n // tt, 2 * tt), o, xf, gate]
    if final:
        in_specs.append(pl.BlockSpec((1, d), lambda i: (0, 0)))
        args.append(final_g)
    return pl.pallas_call(
        functools.partial(_combine_kernel, tt=tt, final=final),
        grid=(n // tt,),
        in_specs=in_specs,
        out_specs=pl.BlockSpec((tt, d), row),
        out_shape=jax.ShapeDtypeStruct((n, d), F32),
        scratch_shapes=[pltpu.SMEM((2 * tt,), jnp.int32), pltpu.VMEM((2, tt, d), F32),
                        pltpu.SemaphoreType.DMA(()), pltpu.SemaphoreType.DMA(())],
        compiler_params=_cp("arbitrary"),
        name="moe_combine",
    )(*args)


def _moe_block(xf, sh, sc, g, gate, rw, w1, w3, w2, seq, final_g=None):
    n = xf.shape[0]
    tm = 1024 if n >= 8192 else 256
    h, route = _router(xf, sh, sc, g, rw, seq)
    src, rp, dest, te, n_active = _route_plan(route, tm)
    hs = _gather_rows(src, h, tm)
    o = _ffn_moe(hs, rp, te, n_active, w1, w3, w2, tm)
    return _combine(dest, o, xf, gate, seq, final_g)


def _rope_swap(w):
    w4 = w.reshape(w.shape[:-1] + (2, 2, QK_ROPE // 4))
    return jnp.stack([-w4[..., 1, :], w4[..., 0, :]], axis=-2).reshape(w.shape)


def _even_layer_weights(mix_in_w, cq_g, ckv_g, w_uq, w_ukv):
    d = mix_in_w.shape[0]
    s0, s1, s2 = Q_LORA, Q_LORA + KV_LORA, Q_LORA + KV_LORA + QK_ROPE
    wcq, wckv, wkr, wpool = mix_in_w[:, :s0], mix_in_w[:, s0:s1], mix_in_w[:, s1:s2], mix_in_w[:, s2:]
    z = lambda k: jnp.zeros((d, k), F32)
    win = jnp.concatenate([wcq, wckv, wpool,
                           z(QK_NOPE), wkr, z(HEAD_PAD - QK_NOPE - QK_ROPE),
                           z(QK_NOPE), _rope_swap(wkr), z(HEAD_PAD - QK_NOPE - QK_ROPE)], axis=1)
    uq = w_uq.reshape(Q_LORA, MLA_HEADS, QK_NOPE + QK_ROPE)
    zq = lambda k: jnp.zeros((Q_LORA, MLA_HEADS, k), F32)
    qa = jnp.concatenate([uq, zq(HEAD_PAD - QK_NOPE - QK_ROPE)], axis=-1)
    qb = jnp.concatenate([zq(QK_NOPE), _rope_swap(uq[..., QK_NOPE:]),
                          zq(HEAD_PAD - QK_NOPE - QK_ROPE)], axis=-1)
    wq = jnp.concatenate([qa.reshape(Q_LORA, -1), qb.reshape(Q_LORA, -1)], axis=1)
    ukv = w_ukv.reshape(KV_LORA, MLA_HEADS, QK_NOPE + V_DIM)
    wk = jnp.concatenate([ukv[..., :QK_NOPE],
                          jnp.zeros((KV_LORA, MLA_HEADS, HEAD_PAD - QK_NOPE), F32)], axis=-1)
    wv = ukv[..., QK_NOPE:]
    wkv = jnp.concatenate([wk.reshape(KV_LORA, -1), wv.reshape(KV_LORA, -1)], axis=1)
    return (win.astype(BF16), cq_g.reshape(1, -1), ckv_g.reshape(1, -1), wq.astype(BF16),
            wkv.astype(BF16))


def _rope_tables(seq):
    rows = seq // GRID_W
    row = jnp.repeat(jnp.arange(rows), GRID_W).astype(F32)
    col = jnp.tile(jnp.arange(GRID_W), rows).astype(F32)
    half = QK_ROPE // 2
    inv = 1.0 / (ROPE_BASE ** (jnp.arange(0, half, 2, dtype=F32) / half))
    ang_r = row[:, None] * inv[None, :]
    ang_c = col[:, None] * inv[None, :]
    cos32 = jnp.concatenate([jnp.cos(ang_r)] * 2 + [jnp.cos(ang_c)] * 2, axis=1)
    sin32 = jnp.concatenate([jnp.sin(ang_r)] * 2 + [jnp.sin(ang_c)] * 2, axis=1)
    return _pack_tables(cos32, sin32)


def _pack_tables(cos32, sin32):
    n = cos32.shape[0]
    pad = HEAD_PAD - QK_NOPE - QK_ROPE
    cos_t = jnp.concatenate([jnp.ones((n, QK_NOPE), F32), cos32, jnp.zeros((n, pad), F32)], axis=1)
    sin_t = jnp.concatenate([jnp.zeros((n, QK_NOPE), F32), sin32, jnp.zeros((n, pad), F32)], axis=1)
    return jnp.concatenate([cos_t * ATTN_SCALE, sin_t * ATTN_SCALE, cos_t, sin_t], axis=1)


def kernel(x, c, ctx, c_ctx, ada_w, ada_b, norm_mix_g, norm_ffn_g, mix_in_w, cq_norm_g, ckv_norm_g,
           w_uq, w_ukv, pool_w, pool_scale, mix_out_w, ffn_w1, ffn_w3, ffn_w2, fnet_out_w,
           router_w, moe_w1, moe_w3, moe_w2, final_g):
    B, L, D = x.shape
    Lc = ctx.shape[1]
    assert ada_w.shape[0] == DEPTH and L % GRID_W == 0
    xf = x.reshape(B * L, D)
    cf = ctx.reshape(B * Lc, D)

    r = ((B + 1 + 7) // 8) * 8
    cvec = jnp.zeros((r, D), F32).at[:B].set(c).at[B].set(c_ctx)
    mod = _adaln(cvec, ada_w, ada_b)

    tab_lat = _rope_tables(L)
    tab_ctx = _pack_tables(jnp.ones((Lc, QK_ROPE), F32), jnp.zeros((Lc, QK_ROPE), F32))
    cos_ch, sin_ch = _dft_tables(D // FNET_GROUPS)
    cs = jnp.asarray(np.concatenate([cos_ch, sin_ch], axis=1), BF16)
    final_g2 = final_g.reshape(1, D)

    for i in range(DEPTH):
        lat = [mod[i, :B, k * D:(k + 1) * D].reshape(B, 1, D) for k in range(6)]
        cx = [jnp.broadcast_to(mod[i, B:B + 1, k * D:(k + 1) * D].reshape(1, 1, D), (B, 1, D))
              for k in range(6)]
        sh1, sc1, g1, sh2, sc2, g2 = lat
        cs1, cc1, cg1, cs2, cc2, cg2 = cx
        gm = norm_mix_g[i].reshape(1, D)
        gf = norm_ffn_g[i].reshape(1, D)
        ctx_needed = any(j % 2 == 0 for j in range(i + 1, DEPTH))
        is_last = i == DEPTH - 1

        if i % 2 == 0:
            e = i // 2
            wts = _even_layer_weights(mix_in_w[e], cq_norm_g[e], ckv_norm_g[e], w_uq[e], w_ukv[e])
            pw = pool_w[e].astype(BF16)
            ps = pool_scale[e].reshape(1, -1)
            wo = mix_out_w[e].astype(BF16)
            wo_a, wo_p = wo[:MLA_HEADS * V_DIM], wo[MLA_HEADS * V_DIM:]
            w1, w3, w2 = ffn_w1[e].astype(BF16), ffn_w3[e].astype(BF16), ffn_w2[e].astype(BF16)

            q_c, k_c, v_c, p_c = _inproj(cf, cs1, cc1, gm, wts, tab_ctx, Lc)
            q_l, k_l, v_l, p_l = _inproj(xf, sh1, sc1, gm, wts, tab_lat, L)
            attn_l = _attention(q_l, [k_l, k_c], [v_l, v_c], B, L, [L, Lc])
            pool_l = _pool(p_l, pw, ps, B, L)
            xf = _resid_mm(xf, g1, [attn_l, pool_l], [wo_a, wo_p], L)
            xf = _ffn_dense(xf, sh2, sc2, gf, g2, w1, w3, w2, L)
            if ctx_needed:
                attn_c = _attention(q_c, [k_c], [v_c], B, Lc, [Lc])
                pool_c = _pool(p_c, pw, ps, B, Lc)
                cf = _resid_mm(cf, cg1, [attn_c, pool_c], [wo_a, wo_p], Lc)
                cf = _ffn_dense(cf, cs2, cc2, gf, cg2, w1, w3, w2, Lc)
        else:
            o = i // 2
            wf = fnet_out_w[o].astype(BF16)
            rw = jnp.zeros((D, 128), F32).at[:, :N_EXPERTS].set(router_w[o])
            w1, w3, w2 = moe_w1[o].astype(BF16), moe_w3[o].astype(BF16), moe_w2[o].astype(BF16)

            cl, sl = _dft_tables(L)
            yc, ys = _fnet1(xf, sh1, sc1, gm, cs, L)
            xf = _fnet2(jnp.asarray(cl, BF16), jnp.asarray(-sl, BF16), yc, ys, xf, g1, wf, B, L)
            xf = _moe_block(xf, sh2, sc2, gf, g2, rw, w1, w3, w2, L,
                            final_g2 if is_last else None)
            if ctx_needed:
                clc, slc = _dft_tables(Lc)
                yc, ys = _fnet1(cf, cs1, cc1, gm, cs, Lc)
                cf = _fnet2(jnp.asarray(clc, BF16), jnp.asarray(-slc, BF16), yc, ys, cf, cg1, wf,
                            B, Lc)
                cf = _moe_block(cf, cs2, cc2, gf, cg2, rw, w1, w3, w2, Lc)

    if DEPTH % 2 == 1:
        raise NotImplementedError("final norm is fused into the last (odd) layer")
    return xf.reshape(B, L, D)
```

```python
import functools
import math

import numpy as np
import jax
import jax.numpy as jnp
from jax import lax
from jax.experimental import pallas as pl
from jax.experimental.pallas import tpu as pltpu

F32 = jnp.float32
BF16 = jnp.bfloat16

DEPTH = 4
GRID_W = 64
MLA_HEADS = 8
QK_NOPE = 64
QK_ROPE = 32
V_DIM = 64
Q_LORA = 256
KV_LORA = 128
ROPE_BASE = 10000.0
ATTN_SCALE = (QK_NOPE + QK_ROPE) ** -0.5
HEAD_PAD = 128
POOL_WINDOWS = (2, 4, 8, 16)
POOL_GC = 128
FNET_GROUPS = 4
N_EXPERTS = 8
EPS = 1e-6

VMEM_LIMIT = 56 * 1024 * 1024


def _cp(*sem):
    return pltpu.CompilerParams(dimension_semantics=sem, vmem_limit_bytes=VMEM_LIMIT)


def _tile(n, pref):
    t = min(n, pref)
    while n % t:
        t //= 2
    assert t % 8 == 0, (n, pref)
    return t


def _rms(x, g):
    return x * lax.rsqrt(jnp.mean(x * x, axis=-1, keepdims=True) + EPS) * g


def _modulate(x, g, sh, sc):
    return _rms(x, g) * (1.0 + sc) + sh


def _silu(a):
    return a / (1.0 + jnp.exp(-a))


def _dot(a, b):
    return jnp.dot(a, b, preferred_element_type=F32)


def _adaln_kernel(c_ref, w_ref, b_ref, o_ref):
    s = _silu(c_ref[...])
    o_ref[...] = jnp.dot(s, w_ref[...], precision=lax.Precision.HIGHEST,
                         preferred_element_type=F32) + b_ref[...]


def _adaln(cvec, ada_w, ada_b):
    depth, d, n6 = ada_w.shape
    r = cvec.shape[0]
    tn = _tile(n6, 1536)
    return pl.pallas_call(
        _adaln_kernel,
        grid=(depth, n6 // tn),
        in_specs=[pl.BlockSpec((r, d), lambda l, j: (0, 0)),
                  pl.BlockSpec((None, d, tn), lambda l, j: (l, 0, j)),
                  pl.BlockSpec((None, 1, tn), lambda l, j: (l, 0, j))],
        out_specs=pl.BlockSpec((None, r, tn), lambda l, j: (l, 0, j)),
        out_shape=jax.ShapeDtypeStruct((depth, r, n6), F32),
        compiler_params=_cp("arbitrary", "arbitrary"),
        name="adaln",
    )(cvec, ada_w, ada_b.reshape(depth, 1, n6))


def _inproj_kernel(x_ref, sh_ref, sc_ref, g_ref, win_ref, gcq_ref, gckv_ref, wq_ref, wkv_ref,
                   tab_ref, q_ref, k_ref, v_ref, p_ref):
    hw = MLA_HEADS * HEAD_PAD
    h = _modulate(x_ref[...], g_ref[...], sh_ref[...], sc_ref[...]).astype(BF16)
    y = _dot(h, win_ref[...])
    cqn = _rms(y[:, 0:Q_LORA], gcq_ref[...]).astype(BF16)
    ckvn = _rms(y[:, Q_LORA:Q_LORA + KV_LORA], gckv_ref[...]).astype(BF16)
    qab = _dot(cqn, wq_ref[...])
    kv = _dot(ckvn, wkv_ref[...])
    tab = tab_ref[...]
    cos_q, sin_q, cos_k, sin_k = (tab[:, i * HEAD_PAD:(i + 1) * HEAD_PAD] for i in range(4))
    kr = y[:, 896:1024] * cos_k + y[:, 1024:1152] * sin_k
    for hd in range(MLA_HEADS):
        lo, hi = hd * HEAD_PAD, (hd + 1) * HEAD_PAD
        q_ref[:, lo:hi] = (qab[:, lo:hi] * cos_q + qab[:, hw + lo:hw + hi] * sin_q).astype(BF16)
        k_ref[:, lo:hi] = (kv[:, lo:hi] + kr).astype(BF16)
    v_ref[...] = kv[:, hw:hw + MLA_HEADS * V_DIM].astype(BF16)
    p_ref[...] = y[:, 384:896]


def _inproj(xf, sh, sc, g, wts, tab, seq):
    n, d = xf.shape
    tm = _tile(seq, 512)
    tpb = seq // tm
    win, gcq, gckv, wq, wkv = wts
    hw = MLA_HEADS * HEAD_PAD
    vw = MLA_HEADS * V_DIM
    row = lambda i: (i, 0)
    per_b = lambda i: (i // tpb, 0, 0)
    full = lambda i: (0, 0)
    return pl.pallas_call(
        _inproj_kernel,
        grid=(n // tm,),
        in_specs=[pl.BlockSpec((tm, d), row),
                  pl.BlockSpec((None, 1, d), per_b),
                  pl.BlockSpec((None, 1, d), per_b),
                  pl.BlockSpec((1, d), full),
                  pl.BlockSpec(win.shape, full),
                  pl.BlockSpec(gcq.shape, full),
                  pl.BlockSpec(gckv.shape, full),
                  pl.BlockSpec(wq.shape, full),
                  pl.BlockSpec(wkv.shape, full),
                  pl.BlockSpec((tm, 4 * HEAD_PAD), lambda i: (i % tpb, 0))],
        out_specs=[pl.BlockSpec((tm, hw), row), pl.BlockSpec((tm, hw), row),
                   pl.BlockSpec((tm, vw), row), pl.BlockSpec((tm, 512), row)],
        out_shape=[jax.ShapeDtypeStruct((n, hw), BF16), jax.ShapeDtypeStruct((n, hw), BF16),
                   jax.ShapeDtypeStruct((n, vw), BF16), jax.ShapeDtypeStruct((n, 512), F32)],
        compiler_params=_cp("arbitrary"),
        name="inproj",
    )(xf, sh, sc, g, win, gcq, gckv, wq, wkv, tab)


def _attn_kernel(*refs, n_chunks):
    q_ref = refs[0]
    k_refs = refs[1:1 + n_chunks]
    v_refs = refs[1 + n_chunks:1 + 2 * n_chunks]
    o_ref = refs[1 + 2 * n_chunks]
    nt = (((1,), (1,)), ((), ()))
    outs = []
    for hh in range(2):
        lo, hi = hh * HEAD_PAD, (hh + 1) * HEAD_PAD
        qh = q_ref[:, lo:hi]
        s = [lax.dot_general(qh, k[:, lo:hi], nt, preferred_element_type=F32) for k in k_refs]
        m = functools.reduce(jnp.maximum, [jnp.max(t, axis=-1, keepdims=True) for t in s])
        p = [jnp.exp(t - m) for t in s]
        l = functools.reduce(jnp.add, [jnp.sum(t, axis=-1, keepdims=True) for t in p])
        o = functools.reduce(jnp.add, [_dot(t.astype(BF16), v[...]) for t, v in zip(p, v_refs)])
        outs.append(o / l)
    lane = lax.broadcasted_iota(jnp.int32, outs[0].shape, 1)
    o_ref[...] = jnp.where(lane < V_DIM, outs[0], outs[1]).astype(o_ref.dtype)


def _attention(q, ks, vs, batch, seq_q, seq_ks):
    n = q.shape[0]
    tq = _tile(seq_q, 512)
    nq = seq_q // tq
    n_chunks = len(ks)
    hp = MLA_HEADS // 2
    in_specs = [pl.BlockSpec((tq, 2 * HEAD_PAD), lambda b, h, i: (b * nq + i, h))]
    in_specs += [pl.BlockSpec((s, 2 * HEAD_PAD), lambda b, h, i: (b, h)) for s in seq_ks]
    in_specs += [pl.BlockSpec((s, 2 * V_DIM), lambda b, h, i: (b, h)) for s in seq_ks]
    return pl.pallas_call(
        functools.partial(_attn_kernel, n_chunks=n_chunks),
        grid=(batch, hp, nq),
        in_specs=in_specs,
        out_specs=pl.BlockSpec((tq, 2 * V_DIM), lambda b, h, i: (b * nq + i, h)),
        out_shape=jax.ShapeDtypeStruct((n, MLA_HEADS * V_DIM), BF16),
        compiler_params=_cp("arbitrary", "arbitrary", "arbitrary"),
        name="attention",
    )(q, *ks, *vs)


def _pool_kernel(p_ref, w_ref, s_ref, o_ref, *, seq):
    row = lax.broadcasted_iota(jnp.int32, (seq, POOL_GC), 0)

    def down(a, s):
        return jnp.where(row >= s, pltpu.roll(a, s, axis=0), 0.0)

    def up(a, s):
        return jnp.where(row < seq - s, pltpu.roll(a, seq - s, axis=0), 0.0)

    for gi, w in enumerate(POOL_WINDOWS):
        lo, hi = gi * POOL_GC, (gi + 1) * POOL_GC
        half = w // 2
        a = p_ref[:, lo:hi]
        d = a
        u = a
        s = 1
        while s < half:
            d = d + down(d, s)
            u = u + up(u, s)
            s *= 2
        win = down(d, 1) + u
        cnt = jnp.minimum(row + half, seq) - jnp.maximum(row - half, 0)
        mixed = (win / cnt.astype(F32) - a).astype(BF16)
        y = _dot(mixed, w_ref[gi]) * s_ref[:, lo:hi]
        o_ref[:, lo:hi] = y.astype(o_ref.dtype)


def _pool(p, pool_w, pool_scale, batch, seq):
    n, pw = p.shape
    return pl.pallas_call(
        functools.partial(_pool_kernel, seq=seq),
        grid=(batch,),
        in_specs=[pl.BlockSpec((seq, pw), lambda b: (b, 0)),
                  pl.BlockSpec(pool_w.shape, lambda b: (0, 0, 0)),
                  pl.BlockSpec((1, pw), lambda b: (0, 0))],
        out_specs=pl.BlockSpec((seq, pw), lambda b: (b, 0)),
        out_shape=jax.ShapeDtypeStruct((n, pw), BF16),
        compiler_params=_cp("arbitrary"),
        name="pool",
    )(p, pool_w, pool_scale)


def _resid_mm_kernel(*refs, n_in):
    x_ref, g_ref = refs[0], refs[1]
    a_refs = refs[2:2 + n_in]
    w_refs = refs[2 + n_in:2 + 2 * n_in]
    o_ref = refs[2 + 2 * n_in]
    acc = functools.reduce(jnp.add, [_dot(a[...], w[...]) for a, w in zip(a_refs, w_refs)])
    o_ref[...] = x_ref[...] + g_ref[...] * acc


def _resid_mm(xf, gate, acts, ws, seq):
    n, d = xf.shape
    tm = _tile(seq, 1024)
    tpb = seq // tm
    row = lambda i: (i, 0)
    in_specs = [pl.BlockSpec((tm, d), row), pl.BlockSpec((None, 1, d), lambda i: (i // tpb, 0, 0))]
    in_specs += [pl.BlockSpec((tm, a.shape[1]), row) for a in acts]
    in_specs += [pl.BlockSpec(w.shape, lambda i: (0, 0)) for w in ws]
    return pl.pallas_call(
        functools.partial(_resid_mm_kernel, n_in=len(acts)),
        grid=(n // tm,),
        in_specs=in_specs,
        out_specs=pl.BlockSpec((tm, d), row),
        out_shape=jax.ShapeDtypeStruct((n, d), F32),
        compiler_params=_cp("arbitrary"),
        name="resid_mm",
    )(xf, gate, *acts, *ws)


def _ffn_dense_kernel(x_ref, sh_ref, sc_ref, g_ref, gate_ref, w1_ref, w3_ref, w2_ref, o_ref,
                      h_s, acc_s):
    j = pl.program_id(1)

    @pl.when(j == 0)
    def _():
        h_s[...] = _modulate(x_ref[...], g_ref[...], sh_ref[...], sc_ref[...]).astype(BF16)
        acc_s[...] = jnp.zeros_like(acc_s)

    h = h_s[...]
    act = _silu(_dot(h, w1_ref[...])) * _dot(h, w3_ref[...])
    acc_s[...] += _dot(act.astype(BF16), w2_ref[...])

    @pl.when(j == pl.num_programs(1) - 1)
    def _():
        o_ref[...] = x_ref[...] + gate_ref[...] * acc_s[...]


def _ffn_dense(xf, sh, sc, g, gate, w1, w3, w2, seq):
    n, d = xf.shape
    f = w1.shape[1]
    tm = _tile(seq, 1024)
    tpb = seq // tm
    tf = 256
    assert f % tf == 0
    row = lambda i, j: (i, 0)
    per_b = lambda i, j: (i // tpb, 0, 0)
    return pl.pallas_call(
        _ffn_dense_kernel,
        grid=(n // tm, f // tf),
        in_specs=[pl.BlockSpec((tm, d), row),
                  pl.BlockSpec((None, 1, d), per_b),
                  pl.BlockSpec((None, 1, d), per_b),
                  pl.BlockSpec((1, d), lambda i, j: (0, 0)),
                  pl.BlockSpec((None, 1, d), per_b),
                  pl.BlockSpec((d, tf), lambda i, j: (0, j)),
                  pl.BlockSpec((d, tf), lambda i, j: (0, j)),
                  pl.BlockSpec((tf, d), lambda i, j: (j, 0))],
        out_specs=pl.BlockSpec((tm, d), row),
        out_shape=jax.ShapeDtypeStruct((n, d), F32),
        scratch_shapes=[pltpu.VMEM((tm, d), BF16), pltpu.VMEM((tm, d), F32)],
        compiler_params=_cp("arbitrary", "arbitrary"),
        name="ffn_dense",
    )(xf, sh, sc, g, gate, w1, w3, w2)


def _fnet1_kernel(x_ref, sh_ref, sc_ref, g_ref, cs_ref, yc_ref, ys_ref):
    h = _modulate(x_ref[...], g_ref[...], sh_ref[...], sc_ref[...]).astype(BF16)
    gc = cs_ref.shape[0]
    for gi in range(FNET_GROUPS):
        lo, hi = gi * gc, (gi + 1) * gc
        y = _dot(h[:, lo:hi], cs_ref[...])
        yc_ref[:, lo:hi] = y[:, :gc].astype(BF16)
        ys_ref[:, lo:hi] = y[:, gc:].astype(BF16)


def _fnet1(xf, sh, sc, g, cs, seq):
    n, d = xf.shape
    tm = _tile(seq, 1024)
    tpb = seq // tm
    row = lambda i: (i, 0)
    per_b = lambda i: (i // tpb, 0, 0)
    return pl.pallas_call(
        _fnet1_kernel,
        grid=(n // tm,),
        in_specs=[pl.BlockSpec((tm, d), row),
                  pl.BlockSpec((None, 1, d), per_b),
                  pl.BlockSpec((None, 1, d), per_b),
                  pl.BlockSpec((1, d), lambda i: (0, 0)),
                  pl.BlockSpec(cs.shape, lambda i: (0, 0))],
        out_specs=[pl.BlockSpec((tm, d), row), pl.BlockSpec((tm, d), row)],
        out_shape=[jax.ShapeDtypeStruct((n, d), BF16), jax.ShapeDtypeStruct((n, d), BF16)],
        compiler_params=_cp("arbitrary"),
        name="fnet_channels",
    )(xf, sh, sc, g, cs)


def _fnet2_kernel(cl_ref, nsl_ref, yc_ref, ys_ref, x_ref, gate_ref, w_ref, o_ref, *, scale):
    re = _dot(cl_ref[...], yc_ref[...]) + _dot(nsl_ref[...], ys_ref[...])
    f = (re * scale).astype(BF16)
    o_ref[...] = x_ref[...] + gate_ref[...] * _dot(f, w_ref[...])


def _fnet2(cl, nsl, yc, ys, xf, gate, w, batch, seq):
    n, d = xf.shape
    tm = _tile(seq, 512)
    nt = seq // tm
    scale = 1.0 / math.sqrt(seq * (d // FNET_GROUPS))
    return pl.pallas_call(
        functools.partial(_fnet2_kernel, scale=scale),
        grid=(batch, nt),
        in_specs=[pl.BlockSpec((tm, seq), lambda b, i: (i, 0)),
                  pl.BlockSpec((tm, seq), lambda b, i: (i, 0)),
                  pl.BlockSpec((seq, d), lambda b, i: (b, 0)),
                  pl.BlockSpec((seq, d), lambda b, i: (b, 0)),
                  pl.BlockSpec((tm, d), lambda b, i: (b * nt + i, 0)),
                  pl.BlockSpec((None, 1, d), lambda b, i: (b, 0, 0)),
                  pl.BlockSpec(w.shape, lambda b, i: (0, 0))],
        out_specs=pl.BlockSpec((tm, d), lambda b, i: (b * nt + i, 0)),
        out_shape=jax.ShapeDtypeStruct((n, d), F32),
        compiler_params=_cp("arbitrary", "arbitrary"),
        name="fnet_positions",
    )(cl, nsl, yc, ys, xf, gate, w)


def _dft_tables(n):
    k = np.arange(n, dtype=np.int64)
    ang = 2.0 * np.pi * ((k[:, None] * k[None, :]) % n).astype(np.float64) / n
    return np.cos(ang).astype(np.float32), np.sin(ang).astype(np.float32)


def _router_kernel(x_ref, sh_ref, sc_ref, g_ref, rw_ref, h_ref, r_ref):
    h = _modulate(x_ref[...], g_ref[...], sh_ref[...], sc_ref[...])
    h_ref[...] = h
    lg = jnp.dot(h, rw_ref[...], precision=lax.Precision.HIGHEST, preferred_element_type=F32)
    lane = lax.broadcasted_iota(jnp.int32, lg.shape, 1).astype(F32)
    neg = jnp.float32(-jnp.inf)
    lg = jnp.where(lane < N_EXPERTS, lg, neg)
    m1 = jnp.max(lg, axis=-1, keepdims=True)
    i1 = jnp.min(jnp.where(lg == m1, lane, 128.0), axis=-1, keepdims=True)
    lg2 = jnp.where(lane == i1, neg, lg)
    m2 = jnp.max(lg2, axis=-1, keepdims=True)
    i2 = jnp.min(jnp.where(lg2 == m2, lane, 128.0), axis=-1, keepdims=True)
    e = jnp.exp(m2 - m1)
    p1 = 1.0 / (1.0 + e)
    p2 = e / (1.0 + e)
    r_ref[...] = jnp.where(lane == 0, i1, jnp.where(lane == 1, i2, jnp.where(
        lane == 2, p1, jnp.where(lane == 3, p2, 0.0))))


def _router(xf, sh, sc, g, rw, seq):
    n, d = xf.shape
    tm = _tile(seq, 512)
    tpb = seq // tm
    row = lambda i: (i, 0)
    per_b = lambda i: (i // tpb, 0, 0)
    return pl.pallas_call(
        _router_kernel,
        grid=(n // tm,),
        in_specs=[pl.BlockSpec((tm, d), row),
                  pl.BlockSpec((None, 1, d), per_b),
                  pl.BlockSpec((None, 1, d), per_b),
                  pl.BlockSpec((1, d), lambda i: (0, 0)),
                  pl.BlockSpec(rw.shape, lambda i: (0, 0))],
        out_specs=[pl.BlockSpec((tm, d), row), pl.BlockSpec((tm, 128), row)],
        out_shape=[jax.ShapeDtypeStruct((n, d), F32), jax.ShapeDtypeStruct((n, 128), F32)],
        compiler_params=_cp("arbitrary"),
        name="router",
    )(xf, sh, sc, g, rw)


IDX_LANES = 128


def _route_plan(route, tm):
    n = route.shape[0]
    ids = route[:, 0:2].astype(jnp.int32).reshape(-1)
    flat = jnp.arange(2 * n, dtype=jnp.int32)
    _, order = lax.sort((ids, flat), num_keys=1, is_stable=True)
    experts = jnp.arange(N_EXPERTS, dtype=jnp.int32)
    counts = jnp.sum((ids[:, None] == experts[None, :]).astype(jnp.int32), axis=0)
    cstart = jnp.cumsum(counts) - counts
    ntile = (counts + tm - 1) // tm
    tend = jnp.cumsum(ntile)
    tstart = tend - ntile
    tiles = (2 * n) // tm + N_EXPERTS
    ti = jnp.arange(tiles, dtype=jnp.int32)
    te = jnp.minimum(jnp.sum((ti[:, None] >= tend[None, :]).astype(jnp.int32), axis=1),
                     N_EXPERTS - 1).astype(jnp.int32)
    within = (ti - tstart[te]) * tm
    active = ti < tend[-1]
    off = jnp.where(active, cstart[te] + within, 0).astype(jnp.int32)
    nvalid = jnp.where(active, jnp.clip(counts[te] - within, 0, tm), 0).astype(jnp.int32)
    pad = tm + 2 * IDX_LANES
    order2d = jnp.concatenate([order, jnp.zeros((pad,), jnp.int32)]).reshape(-1, IDX_LANES)
    return order2d, te, off, nvalid, tend[-1].astype(jnp.int32).reshape(1)


def _ffn_moe_kernel(te_ref, off_ref, nv_ref, na_ref, order_ref, h_hbm, w1_ref, w3_ref, w2_ref,
                    o_hbm, idx_s, gbuf, h_s, acc_s, out_s, sem_i, sem_g, sem_o, *, tm, n_tok):
    i = pl.program_id(0)
    j = pl.program_id(1)
    last = pl.num_programs(1) - 1
    na = na_ref[0]
    active = i < na
    slot = i & 1
    plane = n_tok + tm
    nrow = idx_s.shape[1]
    lane_mask = IDX_LANES - 1
    lane_shift = IDX_LANES.bit_length() - 1

    def load_idx(t, s):
        row0 = lax.shift_right_logical(off_ref[t], lane_shift)
        cp = pltpu.make_async_copy(order_ref.at[pl.ds(row0, nrow)], idx_s.at[s], sem_i)
        cp.start()
        cp.wait()

    def pair_at(t, s, r):
        k = (off_ref[t] & lane_mask) + r
        return idx_s[s, lax.shift_right_logical(k, lane_shift), k & lane_mask]

    def issue_gather(t, s):
        def body(r, carry):
            tok = lax.shift_right_logical(pair_at(t, s, r), 1)
            pltpu.make_async_copy(h_hbm.at[pl.ds(tok, 1)], gbuf.at[s, pl.ds(r, 1)],
                                  sem_g.at[s]).start()
            return carry
        lax.fori_loop(0, tm, body, 0)

    def wait_gather(s):
        pltpu.make_async_copy(h_hbm.at[pl.ds(0, tm)], gbuf.at[s], sem_g.at[s]).wait()

    def issue_scatter(t, s):
        nv = nv_ref[t]

        def body(r, carry):
            f = pair_at(t, s, r)
            dst = jnp.where(r < nv, (f & 1) * plane + lax.shift_right_logical(f, 1), n_tok + r)
            pltpu.make_async_copy(out_s.at[pl.ds(r, 1)], o_hbm.at[pl.ds(dst, 1)], sem_o).start()
            return carry
        lax.fori_loop(0, tm, body, 0)

    def wait_scatter():
        pltpu.make_async_copy(out_s, o_hbm.at[pl.ds(0, tm)], sem_o).wait()

    @pl.when(jnp.logical_and(active, j == 0))
    def _():
        @pl.when(i == 0)
        def _():
            load_idx(0, 0)
            issue_gather(0, 0)
            out_s[...] = jnp.zeros_like(out_s)
            for c in range(2):
                cp = pltpu.make_async_copy(out_s, o_hbm.at[pl.ds(c * plane + n_tok, tm)], sem_o)
                cp.start()
                cp.wait()

        wait_gather(slot)
        h_s[...] = gbuf[slot].astype(BF16)
        acc_s[...] = jnp.zeros_like(acc_s)

        @pl.when(i + 1 < na)
        def _():
            load_idx(i + 1, 1 - slot)
            issue_gather(i + 1, 1 - slot)

    @pl.when(active)
    def _():
        h = h_s[...]
        act = _silu(_dot(h, w1_ref[...])) * _dot(h, w3_ref[...])
        acc_s[...] += _dot(act.astype(BF16), w2_ref[...])

    @pl.when(jnp.logical_and(active, j == last))
    def _():
        @pl.when(i > 0)
        def _():
            wait_scatter()

        out_s[...] = acc_s[...]
        issue_scatter(i, slot)

        @pl.when(i == na - 1)
        def _():
            wait_scatter()


def _ffn_moe(h, order2d, te, off, nvalid, n_active, w1, w3, w2, tm):
    n_tok, d = h.shape
    f = w1.shape[2]
    tf = 512
    assert f % tf == 0 and tm % IDX_LANES == 0
    nj = f // tf
    tiles = te.shape[0]
    nrow = tm // IDX_LANES + 1

    def jj(i, j, na):
        return jnp.where(i < na[0], j, nj - 1)

    grid_spec = pltpu.PrefetchScalarGridSpec(
        num_scalar_prefetch=4,
        grid=(tiles, nj),
        in_specs=[pl.BlockSpec(order2d.shape, lambda i, j, te, off, nv, na: (0, 0)),
                  pl.BlockSpec(memory_space=pl.ANY),
                  pl.BlockSpec((None, d, tf), lambda i, j, te, off, nv, na: (te[i], 0, jj(i, j, na))),
                  pl.BlockSpec((None, d, tf), lambda i, j, te, off, nv, na: (te[i], 0, jj(i, j, na))),
                  pl.BlockSpec((None, tf, d), lambda i, j, te, off, nv, na: (te[i], jj(i, j, na), 0))],
        out_specs=pl.BlockSpec(memory_space=pl.ANY),
        scratch_shapes=[pltpu.SMEM((2, nrow, IDX_LANES), jnp.int32),
                        pltpu.VMEM((2, tm, d), F32),
                        pltpu.VMEM((tm, d), BF16),
                        pltpu.VMEM((tm, d), F32),
                        pltpu.VMEM((tm, d), F32),
                        pltpu.SemaphoreType.DMA(()),
                        pltpu.SemaphoreType.DMA((2,)),
                        pltpu.SemaphoreType.DMA(())],
    )
    return pl.pallas_call(
        functools.partial(_ffn_moe_kernel, tm=tm, n_tok=n_tok),
        grid_spec=grid_spec,
        out_shape=jax.ShapeDtypeStruct((2 * (n_tok + tm), d), F32),
        compiler_params=_cp("arbitrary", "arbitrary"),
        name="ffn_moe",
    )(te, off, nvalid, n_active, order2d, h, w1, w3, w2)


def _combine_kernel(*refs, final):
    if final:
        x_ref, gate_ref, r_ref, oa_ref, ob_ref, fg_ref, out_ref = refs
    else:
        x_ref, gate_ref, r_ref, oa_ref, ob_ref, out_ref = refs
    r = r_ref[...]
    y = x_ref[...] + gate_ref[...] * (r[:, 2:3] * oa_ref[...] + r[:, 3:4] * ob_ref[...])
    if final:
        y = _rms(y, fg_ref[...])
    out_ref[...] = y


def _combine(o3, route, xf, gate, seq, final_g=None):
    n, d = xf.shape
    tt = _tile(seq, 512)
    tpb = seq // tt
    final = final_g is not None
    row = lambda i: (i, 0)
    in_specs = [pl.BlockSpec((tt, d), row),
                pl.BlockSpec((None, 1, d), lambda i: (i // tpb, 0, 0)),
                pl.BlockSpec((tt, route.shape[1]), row),
                pl.BlockSpec((None, tt, d), lambda i: (0, i, 0)),
                pl.BlockSpec((None, tt, d), lambda i: (1, i, 0))]
    args = [xf, gate, route, o3, o3]
    if final:
        in_specs.append(pl.BlockSpec((1, d), lambda i: (0, 0)))
        args.append(final_g)
    return pl.pallas_call(
        functools.partial(_combine_kernel, final=final),
        grid=(n // tt,),
        in_specs=in_specs,
        out_specs=pl.BlockSpec((tt, d), row),
        out_shape=jax.ShapeDtypeStruct((n, d), F32),
        compiler_params=_cp("arbitrary"),
        name="moe_combine",
    )(*args)


def _moe_block(xf, sh, sc, g, gate, rw, w1, w3, w2, seq, final_g=None):
    n, d = xf.shape
    tm = 1024 if n >= 32768 else (512 if n >= 4096 else 256)
    h, route = _router(xf, sh, sc, g, rw, seq)
    order2d, te, off, nvalid, n_active = _route_plan(route, tm)
    o = _ffn_moe(h, order2d, te, off, nvalid, n_active, w1, w3, w2, tm)
    return _combine(o.reshape(2, n + tm, d), route, xf, gate, seq, final_g)


def _rope_swap(w):
    w4 = w.reshape(w.shape[:-1] + (2, 2, QK_ROPE // 4))
    return jnp.stack([-w4[..., 1, :], w4[..., 0, :]], axis=-2).reshape(w.shape)


def _even_layer_weights(mix_in_w, cq_g, ckv_g, w_uq, w_ukv):
    d = mix_in_w.shape[0]
    s0, s1, s2 = Q_LORA, Q_LORA + KV_LORA, Q_LORA + KV_LORA + QK_ROPE
    wcq, wckv, wkr, wpool = mix_in_w[:, :s0], mix_in_w[:, s0:s1], mix_in_w[:, s1:s2], mix_in_w[:, s2:]
    z = lambda k: jnp.zeros((d, k), F32)
    win = jnp.concatenate([wcq, wckv, wpool,
                           z(QK_NOPE), wkr, z(HEAD_PAD - QK_NOPE - QK_ROPE),
                           z(QK_NOPE), _rope_swap(wkr), z(HEAD_PAD - QK_NOPE - QK_ROPE)], axis=1)
    uq = w_uq.reshape(Q_LORA, MLA_HEADS, QK_NOPE + QK_ROPE)
    zq = lambda k: jnp.zeros((Q_LORA, MLA_HEADS, k), F32)
    qa = jnp.concatenate([uq, zq(HEAD_PAD - QK_NOPE - QK_ROPE)], axis=-1)
    qb = jnp.concatenate([zq(QK_NOPE), _rope_swap(uq[..., QK_NOPE:]),
                          zq(HEAD_PAD - QK_NOPE - QK_ROPE)], axis=-1)
    wq = jnp.concatenate([qa.reshape(Q_LORA, -1), qb.reshape(Q_LORA, -1)], axis=1)
    ukv = w_ukv.reshape(KV_LORA, MLA_HEADS, QK_NOPE + V_DIM)
    wk = jnp.concatenate([ukv[..., :QK_NOPE],
                          jnp.zeros((KV_LORA, MLA_HEADS, HEAD_PAD - QK_NOPE), F32)], axis=-1)
    wv = ukv[..., QK_NOPE:]
    wkv = jnp.concatenate([wk.reshape(KV_LORA, -1), wv.reshape(KV_LORA, -1)], axis=1)
    return (win.astype(BF16), cq_g.reshape(1, -1), ckv_g.reshape(1, -1), wq.astype(BF16),
            wkv.astype(BF16))


def _rope_tables(seq):
    rows = seq // GRID_W
    row = jnp.repeat(jnp.arange(rows), GRID_W).astype(F32)
    col = jnp.tile(jnp.arange(GRID_W), rows).astype(F32)
    half = QK_ROPE // 2
    inv = 1.0 / (ROPE_BASE ** (jnp.arange(0, half, 2, dtype=F32) / half))
    ang_r = row[:, None] * inv[None, :]
    ang_c = col[:, None] * inv[None, :]
    cos32 = jnp.concatenate([jnp.cos(ang_r)] * 2 + [jnp.cos(ang_c)] * 2, axis=1)
    sin32 = jnp.concatenate([jnp.sin(ang_r)] * 2 + [jnp.sin(ang_c)] * 2, axis=1)
    return _pack_tables(cos32, sin32)


def _pack_tables(cos32, sin32):
    n = cos32.shape[0]
    pad = HEAD_PAD - QK_NOPE - QK_ROPE
    cos_t = jnp.concatenate([jnp.ones((n, QK_NOPE), F32), cos32, jnp.zeros((n, pad), F32)], axis=1)
    sin_t = jnp.concatenate([jnp.zeros((n, QK_NOPE), F32), sin32, jnp.zeros((n, pad), F32)], axis=1)
    return jnp.concatenate([cos_t * ATTN_SCALE, sin_t * ATTN_SCALE, cos_t, sin_t], axis=1)


def kernel(x, c, ctx, c_ctx, ada_w, ada_b, norm_mix_g, norm_ffn_g, mix_in_w, cq_norm_g, ckv_norm_g,
           w_uq, w_ukv, pool_w, pool_scale, mix_out_w, ffn_w1, ffn_w3, ffn_w2, fnet_out_w,
           router_w, moe_w1, moe_w3, moe_w2, final_g):
    B, L, D = x.shape
    Lc = ctx.shape[1]
    assert ada_w.shape[0] == DEPTH and L % GRID_W == 0
    xf = x.reshape(B * L, D)
    cf = ctx.reshape(B * Lc, D)

    r = ((B + 1 + 7) // 8) * 8
    cvec = jnp.zeros((r, D), F32).at[:B].set(c).at[B].set(c_ctx)
    mod = _adaln(cvec, ada_w, ada_b)

    tab_lat = _rope_tables(L)
    tab_ctx = _pack_tables(jnp.ones((Lc, QK_ROPE), F32), jnp.zeros((Lc, QK_ROPE), F32))
    cos_ch, sin_ch = _dft_tables(D // FNET_GROUPS)
    cs = jnp.asarray(np.concatenate([cos_ch, sin_ch], axis=1), BF16)
    final_g2 = final_g.reshape(1, D)

    for i in range(DEPTH):
        lat = [mod[i, :B, k * D:(k + 1) * D].reshape(B, 1, D) for k in range(6)]
        cx = [jnp.broadcast_to(mod[i, B:B + 1, k * D:(k + 1) * D].reshape(1, 1, D), (B, 1, D))
              for k in range(6)]
        sh1, sc1, g1, sh2, sc2, g2 = lat
        cs1, cc1, cg1, cs2, cc2, cg2 = cx
        gm = norm_mix_g[i].reshape(1, D)
        gf = norm_ffn_g[i].reshape(1, D)
        ctx_needed = any(j % 2 == 0 for j in range(i + 1, DEPTH))
        is_last = i == DEPTH - 1

        if i % 2 == 0:
            e = i // 2
            wts = _even_layer_weights(mix_in_w[e], cq_norm_g[e], ckv_norm_g[e], w_uq[e], w_ukv[e])
            pw = pool_w[e].astype(BF16)
            ps = pool_scale[e].reshape(1, -1)
            wo = mix_out_w[e].astype(BF16)
            wo_a, wo_p = wo[:MLA_HEADS * V_DIM], wo[MLA_HEADS * V_DIM:]
            w1, w3, w2 = ffn_w1[e].astype(BF16), ffn_w3[e].astype(BF16), ffn_w2[e].astype(BF16)

            q_c, k_c, v_c, p_c = _inproj(cf, cs1, cc1, gm, wts, tab_ctx, Lc)
            q_l, k_l, v_l, p_l = _inproj(xf, sh1, sc1, gm, wts, tab_lat, L)
            attn_l = _attention(q_l, [k_l, k_c], [v_l, v_c], B, L, [L, Lc])
            pool_l = _pool(p_l, pw, ps, B, L)
            xf = _resid_mm(xf, g1, [attn_l, pool_l], [wo_a, wo_p], L)
            xf = _ffn_dense(xf, sh2, sc2, gf, g2, w1, w3, w2, L)
            if ctx_needed:
                attn_c = _attention(q_c, [k_c], [v_c], B, Lc, [Lc])
                pool_c = _pool(p_c, pw, ps, B, Lc)
                cf = _resid_mm(cf, cg1, [attn_c, pool_c], [wo_a, wo_p], Lc)
                cf = _ffn_dense(cf, cs2, cc2, gf, cg2, w1, w3, w2, Lc)
        else:
            o = i // 2
            wf = fnet_out_w[o].astype(BF16)
            rw = jnp.zeros((D, 128), F32).at[:, :N_EXPERTS].set(router_w[o])
            w1, w3, w2 = moe_w1[o].astype(BF16), moe_w3[o].astype(BF16), moe_w2[o].astype(BF16)

            cl, sl = _dft_tables(L)
            yc, ys = _fnet1(xf, sh1, sc1, gm, cs, L)
            xf = _fnet2(jnp.asarray(cl, BF16), jnp.asarray(-sl, BF16), yc, ys, xf, g1, wf, B, L)
            xf = _moe_block(xf, sh2, sc2, gf, g2, rw, w1, w3, w2, L,
                            final_g2 if is_last else None)
            if ctx_needed:
                clc, slc = _dft_tables(Lc)
                yc, ys = _fnet1(cf, cs1, cc1, gm, cs, Lc)
                cf = _fnet2(jnp.asarray(clc, BF16), jnp.asarray(-slc, BF16), yc, ys, cf, cg1, wf,
                            B, Lc)
                cf = _moe_block(cf, cs2, cc2, gf, cg2, rw, w1, w3, w2, Lc)

    if DEPTH % 2 == 1:
        raise NotImplementedError("final norm is fused into the last (odd) layer")
    return xf.reshape(B, L, D)
```

```python
import functools
import math

import numpy as np
import jax
import jax.numpy as jnp
from jax import lax
from jax.experimental import pallas as pl
from jax.experimental.pallas import tpu as pltpu

F32 = jnp.float32
BF16 = jnp.bfloat16

DEPTH = 4
GRID_W = 64
MLA_HEADS = 8
QK_NOPE = 64
QK_ROPE = 32
V_DIM = 64
Q_LORA = 256
KV_LORA = 128
ROPE_BASE = 10000.0
ATTN_SCALE = (QK_NOPE + QK_ROPE) ** -0.5
HEAD_PAD = 128
POOL_WINDOWS = (2, 4, 8, 16)
POOL_GC = 128
FNET_GROUPS = 4
N_EXPERTS = 8
EPS = 1e-6

VMEM_LIMIT = 56 * 1024 * 1024


def _cp(*sem):
    return pltpu.CompilerParams(dimension_semantics=sem, vmem_limit_bytes=VMEM_LIMIT)


def _tile(n, pref):
    t = min(n, pref)
    while n % t:
        t //= 2
    assert t % 8 == 0, (n, pref)
    return t


def _rms(x, g):
    return x * lax.rsqrt(jnp.mean(x * x, axis=-1, keepdims=True) + EPS) * g


def _modulate(x, g, sh, sc):
    return _rms(x, g) * (1.0 + sc) + sh


def _silu(a):
    return a / (1.0 + jnp.exp(-a))


def _dot(a, b):
    return jnp.dot(a, b, preferred_element_type=F32)


def _adaln_kernel(c_ref, w_ref, b_ref, o_ref):
    s = _silu(c_ref[...])
    o_ref[...] = jnp.dot(s, w_ref[...], precision=lax.Precision.HIGHEST,
                         preferred_element_type=F32) + b_ref[...]


def _adaln(cvec, ada_w, ada_b):
    depth, d, n6 = ada_w.shape
    r = cvec.shape[0]
    tn = _tile(n6, 1536)
    return pl.pallas_call(
        _adaln_kernel,
        grid=(depth, n6 // tn),
        in_specs=[pl.BlockSpec((r, d), lambda l, j: (0, 0)),
                  pl.BlockSpec((None, d, tn), lambda l, j: (l, 0, j)),
                  pl.BlockSpec((None, 1, tn), lambda l, j: (l, 0, j))],
        out_specs=pl.BlockSpec((None, r, tn), lambda l, j: (l, 0, j)),
        out_shape=jax.ShapeDtypeStruct((depth, r, n6), F32),
        compiler_params=_cp("arbitrary", "arbitrary"),
        name="adaln",
    )(cvec, ada_w, ada_b.reshape(depth, 1, n6))


def _inproj_kernel(x_ref, sh_ref, sc_ref, g_ref, win_ref, gcq_ref, gckv_ref, wq_ref, wkv_ref,
                   tab_ref, q_ref, k_ref, v_ref, p_ref):
    hw = MLA_HEADS * HEAD_PAD
    h = _modulate(x_ref[...], g_ref[...], sh_ref[...], sc_ref[...]).astype(BF16)
    y = _dot(h, win_ref[...])
    cqn = _rms(y[:, 0:Q_LORA], gcq_ref[...]).astype(BF16)
    ckvn = _rms(y[:, Q_LORA:Q_LORA + KV_LORA], gckv_ref[...]).astype(BF16)
    qab = _dot(cqn, wq_ref[...])
    kv = _dot(ckvn, wkv_ref[...])
    tab = tab_ref[...]
    cos_q, sin_q, cos_k, sin_k = (tab[:, i * HEAD_PAD:(i + 1) * HEAD_PAD] for i in range(4))
    kr = y[:, 896:1024] * cos_k + y[:, 1024:1152] * sin_k
    for hd in range(MLA_HEADS):
        lo, hi = hd * HEAD_PAD, (hd + 1) * HEAD_PAD
        q_ref[:, lo:hi] = (qab[:, lo:hi] * cos_q + qab[:, hw + lo:hw + hi] * sin_q).astype(BF16)
        k_ref[:, lo:hi] = (kv[:, lo:hi] + kr).astype(BF16)
    lane = lax.broadcasted_iota(jnp.int32, (1, hw), 1)
    ones = jnp.where((lane & (HEAD_PAD - 1)) >= V_DIM, 1.0, 0.0)
    v_ref[...] = (kv[:, hw:2 * hw] + ones).astype(BF16)
    p_ref[...] = y[:, 384:896]


def _inproj(xf, sh, sc, g, wts, tab, seq):
    n, d = xf.shape
    tm = _tile(seq, 512)
    tpb = seq // tm
    win, gcq, gckv, wq, wkv = wts
    hw = MLA_HEADS * HEAD_PAD
    vw = hw
    row = lambda i: (i, 0)
    per_b = lambda i: (i // tpb, 0, 0)
    full = lambda i: (0, 0)
    return pl.pallas_call(
        _inproj_kernel,
        grid=(n // tm,),
        in_specs=[pl.BlockSpec((tm, d), row),
                  pl.BlockSpec((None, 1, d), per_b),
                  pl.BlockSpec((None, 1, d), per_b),
                  pl.BlockSpec((1, d), full),
                  pl.BlockSpec(win.shape, full),
                  pl.BlockSpec(gcq.shape, full),
                  pl.BlockSpec(gckv.shape, full),
                  pl.BlockSpec(wq.shape, full),
                  pl.BlockSpec(wkv.shape, full),
                  pl.BlockSpec((tm, 4 * HEAD_PAD), lambda i: (i % tpb, 0))],
        out_specs=[pl.BlockSpec((tm, hw), row), pl.BlockSpec((tm, hw), row),
                   pl.BlockSpec((tm, vw), row), pl.BlockSpec((tm, 512), row)],
        out_shape=[jax.ShapeDtypeStruct((n, hw), BF16), jax.ShapeDtypeStruct((n, hw), BF16),
                   jax.ShapeDtypeStruct((n, vw), BF16), jax.ShapeDtypeStruct((n, 512), F32)],
        compiler_params=_cp("arbitrary"),
        name="inproj",
    )(xf, sh, sc, g, win, gcq, gckv, wq, wkv, tab)


def _attn_kernel(*refs, n_chunks):
    q_ref = refs[0]
    k_refs = refs[1:1 + n_chunks]
    v_refs = refs[1 + n_chunks:1 + 2 * n_chunks]
    o_ref = refs[1 + 2 * n_chunks]
    nt = (((1,), (1,)), ((), ()))
    for hh in range(2):
        lo, hi = hh * HEAD_PAD, (hh + 1) * HEAD_PAD
        qh = q_ref[:, lo:hi]
        s = [lax.dot_general(qh, k[:, lo:hi], nt, preferred_element_type=F32) for k in k_refs]
        m = functools.reduce(jnp.maximum, [jnp.max(t, axis=-1, keepdims=True) for t in s])
        o = functools.reduce(jnp.add, [_dot(jnp.exp2(t - m).astype(BF16), v[:, lo:hi])
                                       for t, v in zip(s, v_refs)])
        o_ref[:, lo:hi] = (o / o[:, V_DIM:V_DIM + 1]).astype(o_ref.dtype)


def _attention(q, ks, vs, batch, seq_q, seq_ks):
    n = q.shape[0]
    tq = _tile(seq_q, 512)
    nq = seq_q // tq
    n_chunks = len(ks)
    hp = MLA_HEADS // 2
    in_specs = [pl.BlockSpec((tq, 2 * HEAD_PAD), lambda b, h, i: (b * nq + i, h))]
    in_specs += [pl.BlockSpec((s, 2 * HEAD_PAD), lambda b, h, i: (b, h)) for s in seq_ks]
    in_specs += [pl.BlockSpec((s, 2 * HEAD_PAD), lambda b, h, i: (b, h)) for s in seq_ks]
    return pl.pallas_call(
        functools.partial(_attn_kernel, n_chunks=n_chunks),
        grid=(batch, hp, nq),
        in_specs=in_specs,
        out_specs=pl.BlockSpec((tq, 2 * HEAD_PAD), lambda b, h, i: (b * nq + i, h)),
        out_shape=jax.ShapeDtypeStruct((n, MLA_HEADS * HEAD_PAD), BF16),
        compiler_params=_cp("arbitrary", "arbitrary", "arbitrary"),
        name="attention",
    )(q, *ks, *vs)


def _pool_kernel(p_ref, w_ref, s_ref, o_ref, *, seq):
    row = lax.broadcasted_iota(jnp.int32, (seq, POOL_GC), 0)

    def down(a, s):
        return jnp.where(row >= s, pltpu.roll(a, s, axis=0), 0.0)

    def up(a, s):
        return jnp.where(row < seq - s, pltpu.roll(a, seq - s, axis=0), 0.0)

    for gi, w in enumerate(POOL_WINDOWS):
        lo, hi = gi * POOL_GC, (gi + 1) * POOL_GC
        half = w // 2
        a = p_ref[:, lo:hi]
        d = a
        u = a
        s = 1
        while s < half:
            d = d + down(d, s)
            u = u + up(u, s)
            s *= 2
        win = down(d, 1) + u
        cnt = jnp.minimum(row + half, seq) - jnp.maximum(row - half, 0)
        mixed = (win / cnt.astype(F32) - a).astype(BF16)
        y = _dot(mixed, w_ref[gi]) * s_ref[:, lo:hi]
        o_ref[:, lo:hi] = y.astype(o_ref.dtype)


def _pool(p, pool_w, pool_scale, batch, seq):
    n, pw = p.shape
    return pl.pallas_call(
        functools.partial(_pool_kernel, seq=seq),
        grid=(batch,),
        in_specs=[pl.BlockSpec((seq, pw), lambda b: (b, 0)),
                  pl.BlockSpec(pool_w.shape, lambda b: (0, 0, 0)),
                  pl.BlockSpec((1, pw), lambda b: (0, 0))],
        out_specs=pl.BlockSpec((seq, pw), lambda b: (b, 0)),
        out_shape=jax.ShapeDtypeStruct((n, pw), BF16),
        compiler_params=_cp("arbitrary"),
        name="pool",
    )(p, pool_w, pool_scale)


def _resid_mm_kernel(*refs, n_in):
    x_ref, g_ref = refs[0], refs[1]
    a_refs = refs[2:2 + n_in]
    w_refs = refs[2 + n_in:2 + 2 * n_in]
    o_ref = refs[2 + 2 * n_in]
    acc = functools.reduce(jnp.add, [_dot(a[...], w[...]) for a, w in zip(a_refs, w_refs)])
    o_ref[...] = x_ref[...] + g_ref[...] * acc


def _resid_mm(xf, gate, acts, ws, seq):
    n, d = xf.shape
    tm = _tile(seq, 1024)
    tpb = seq // tm
    row = lambda i: (i, 0)
    in_specs = [pl.BlockSpec((tm, d), row), pl.BlockSpec((None, 1, d), lambda i: (i // tpb, 0, 0))]
    in_specs += [pl.BlockSpec((tm, a.shape[1]), row) for a in acts]
    in_specs += [pl.BlockSpec(w.shape, lambda i: (0, 0)) for w in ws]
    return pl.pallas_call(
        functools.partial(_resid_mm_kernel, n_in=len(acts)),
        grid=(n // tm,),
        in_specs=in_specs,
        out_specs=pl.BlockSpec((tm, d), row),
        out_shape=jax.ShapeDtypeStruct((n, d), F32),
        compiler_params=_cp("arbitrary"),
        name="resid_mm",
    )(xf, gate, *acts, *ws)


def _ffn_dense_kernel(x_ref, sh_ref, sc_ref, g_ref, gate_ref, w1_ref, w3_ref, w2_ref, o_ref,
                      h_s, acc_s):
    j = pl.program_id(1)

    @pl.when(j == 0)
    def _():
        h_s[...] = _modulate(x_ref[...], g_ref[...], sh_ref[...], sc_ref[...]).astype(BF16)
        acc_s[...] = jnp.zeros_like(acc_s)

    h = h_s[...]
    act = _silu(_dot(h, w1_ref[...])) * _dot(h, w3_ref[...])
    acc_s[...] += _dot(act.astype(BF16), w2_ref[...])

    @pl.when(j == pl.num_programs(1) - 1)
    def _():
        o_ref[...] = x_ref[...] + gate_ref[...] * acc_s[...]


def _ffn_dense(xf, sh, sc, g, gate, w1, w3, w2, seq):
    n, d = xf.shape
    f = w1.shape[1]
    tm = _tile(seq, 1024)
    tpb = seq // tm
    tf = 256
    assert f % tf == 0
    row = lambda i, j: (i, 0)
    per_b = lambda i, j: (i // tpb, 0, 0)
    return pl.pallas_call(
        _ffn_dense_kernel,
        grid=(n // tm, f // tf),
        in_specs=[pl.BlockSpec((tm, d), row),
                  pl.BlockSpec((None, 1, d), per_b),
                  pl.BlockSpec((None, 1, d), per_b),
                  pl.BlockSpec((1, d), lambda i, j: (0, 0)),
                  pl.BlockSpec((None, 1, d), per_b),
                  pl.BlockSpec((d, tf), lambda i, j: (0, j)),
                  pl.BlockSpec((d, tf), lambda i, j: (0, j)),
                  pl.BlockSpec((tf, d), lambda i, j: (j, 0))],
        out_specs=pl.BlockSpec((tm, d), row),
        out_shape=jax.ShapeDtypeStruct((n, d), F32),
        scratch_shapes=[pltpu.VMEM((tm, d), BF16), pltpu.VMEM((tm, d), F32)],
        compiler_params=_cp("arbitrary", "arbitrary"),
        name="ffn_dense",
    )(xf, sh, sc, g, gate, w1, w3, w2)


def _fnet1_kernel(x_ref, sh_ref, sc_ref, g_ref, cs_ref, yc_ref, ys_ref):
    h = _modulate(x_ref[...], g_ref[...], sh_ref[...], sc_ref[...]).astype(BF16)
    gc = cs_ref.shape[0]
    for gi in range(FNET_GROUPS):
        lo, hi = gi * gc, (gi + 1) * gc
        y = _dot(h[:, lo:hi], cs_ref[...])
        yc_ref[:, lo:hi] = y[:, :gc].astype(BF16)
        ys_ref[:, lo:hi] = y[:, gc:].astype(BF16)


def _fnet1(xf, sh, sc, g, cs, seq):
    n, d = xf.shape
    tm = _tile(seq, 1024)
    tpb = seq // tm
    row = lambda i: (i, 0)
    per_b = lambda i: (i // tpb, 0, 0)
    return pl.pallas_call(
        _fnet1_kernel,
        grid=(n // tm,),
        in_specs=[pl.BlockSpec((tm, d), row),
                  pl.BlockSpec((None, 1, d), per_b),
                  pl.BlockSpec((None, 1, d), per_b),
                  pl.BlockSpec((1, d), lambda i: (0, 0)),
                  pl.BlockSpec(cs.shape, lambda i: (0, 0))],
        out_specs=[pl.BlockSpec((tm, d), row), pl.BlockSpec((tm, d), row)],
        out_shape=[jax.ShapeDtypeStruct((n, d), BF16), jax.ShapeDtypeStruct((n, d), BF16)],
        compiler_params=_cp("arbitrary"),
        name="fnet_channels",
    )(xf, sh, sc, g, cs)


def _fnet2_kernel(cl_ref, nsl_ref, yc_ref, ys_ref, x_ref, gate_ref, w_ref, o_ref, *, scale):
    re = _dot(cl_ref[...], yc_ref[...]) + _dot(nsl_ref[...], ys_ref[...])
    f = (re * scale).astype(BF16)
    o_ref[...] = x_ref[...] + gate_ref[...] * _dot(f, w_ref[...])


def _fnet2(cl, nsl, yc, ys, xf, gate, w, batch, seq):
    n, d = xf.shape
    tm = _tile(seq, 512)
    nt = seq // tm
    scale = 1.0 / math.sqrt(seq * (d // FNET_GROUPS))
    return pl.pallas_call(
        functools.partial(_fnet2_kernel, scale=scale),
        grid=(batch, nt),
        in_specs=[pl.BlockSpec((tm, seq), lambda b, i: (i, 0)),
                  pl.BlockSpec((tm, seq), lambda b, i: (i, 0)),
                  pl.BlockSpec((seq, d), lambda b, i: (b, 0)),
                  pl.BlockSpec((seq, d), lambda b, i: (b, 0)),
                  pl.BlockSpec((tm, d), lambda b, i: (b * nt + i, 0)),
                  pl.BlockSpec((None, 1, d), lambda b, i: (b, 0, 0)),
                  pl.BlockSpec(w.shape, lambda b, i: (0, 0))],
        out_specs=pl.BlockSpec((tm, d), lambda b, i: (b * nt + i, 0)),
        out_shape=jax.ShapeDtypeStruct((n, d), F32),
        compiler_params=_cp("arbitrary", "arbitrary"),
        name="fnet_positions",
    )(cl, nsl, yc, ys, xf, gate, w)


def _dft_tables(n):
    k = np.arange(n, dtype=np.int64)
    ang = 2.0 * np.pi * ((k[:, None] * k[None, :]) % n).astype(np.float64) / n
    return np.cos(ang).astype(np.float32), np.sin(ang).astype(np.float32)


def _router_kernel(x_ref, sh_ref, sc_ref, g_ref, rw_ref, h_ref, r_ref):
    h = _modulate(x_ref[...], g_ref[...], sh_ref[...], sc_ref[...])
    h_ref[...] = h
    lg = jnp.dot(h, rw_ref[...], precision=lax.Precision.HIGHEST, preferred_element_type=F32)
    lane = lax.broadcasted_iota(jnp.int32, lg.shape, 1).astype(F32)
    neg = jnp.float32(-jnp.inf)
    lg = jnp.where(lane < N_EXPERTS, lg, neg)
    m1 = jnp.max(lg, axis=-1, keepdims=True)
    i1 = jnp.min(jnp.where(lg == m1, lane, 128.0), axis=-1, keepdims=True)
    lg2 = jnp.where(lane == i1, neg, lg)
    m2 = jnp.max(lg2, axis=-1, keepdims=True)
    i2 = jnp.min(jnp.where(lg2 == m2, lane, 128.0), axis=-1, keepdims=True)
    e = jnp.exp(m2 - m1)
    p1 = 1.0 / (1.0 + e)
    p2 = e / (1.0 + e)
    r_ref[...] = jnp.where(lane == 0, i1, jnp.where(lane == 1, i2, jnp.where(
        lane == 2, p1, jnp.where(lane == 3, p2, 0.0))))


def _router(xf, sh, sc, g, rw, seq):
    n, d = xf.shape
    tm = _tile(seq, 512)
    tpb = seq // tm
    row = lambda i: (i, 0)
    per_b = lambda i: (i // tpb, 0, 0)
    return pl.pallas_call(
        _router_kernel,
        grid=(n // tm,),
        in_specs=[pl.BlockSpec((tm, d), row),
                  pl.BlockSpec((None, 1, d), per_b),
                  pl.BlockSpec((None, 1, d), per_b),
                  pl.BlockSpec((1, d), lambda i: (0, 0)),
                  pl.BlockSpec(rw.shape, lambda i: (0, 0))],
        out_specs=[pl.BlockSpec((tm, d), row), pl.BlockSpec((tm, 128), row)],
        out_shape=[jax.ShapeDtypeStruct((n, d), F32), jax.ShapeDtypeStruct((n, 128), F32)],
        compiler_params=_cp("arbitrary"),
        name="router",
    )(xf, sh, sc, g, rw)


MOE_TF = 512
SUBLANES = 8
LANES = 128


def _moe_tiling(tm, nj):
    chunk = -(-tm // nj)
    chunk = -(-chunk // SUBLANES) * SUBLANES
    tmx = chunk * nj
    return chunk, tmx, -(-tmx // LANES) * LANES


def _route_plan(route, tm, nj):
    n = route.shape[0]
    _, tmx, width = _moe_tiling(tm, nj)
    plane = n + tmx
    ids = route[:, 0:2].astype(jnp.int32).reshape(-1)
    flat = jnp.arange(2 * n, dtype=jnp.int32)
    _, order = lax.sort((ids, flat), num_keys=1, is_stable=True)
    experts = jnp.arange(N_EXPERTS, dtype=jnp.int32)
    counts = jnp.sum((ids[:, None] == experts[None, :]).astype(jnp.int32), axis=0)
    cstart = jnp.cumsum(counts) - counts
    ntile = (counts + tm - 1) // tm
    tend = jnp.cumsum(ntile)
    tstart = tend - ntile
    tiles = (2 * n) // tm + N_EXPERTS + 2
    ti = jnp.arange(tiles, dtype=jnp.int32)
    te = jnp.minimum(jnp.sum((ti[:, None] >= tend[None, :]).astype(jnp.int32), axis=1),
                     N_EXPERTS - 1).astype(jnp.int32)
    within = (ti - tstart[te]) * tm
    real = ti < tend[-1]
    off = jnp.where(real, cstart[te] + within, 0).astype(jnp.int32)
    nvalid = jnp.where(real, jnp.clip(counts[te] - within, 0, tm), 0).astype(jnp.int32)
    order_p = jnp.concatenate([order, jnp.zeros((width,), jnp.int32)])
    pairs = jax.vmap(lambda o: lax.dynamic_slice(order_p, (o,), (width,)))(off)
    r = jnp.arange(width, dtype=jnp.int32)
    spare = (n + jnp.minimum(r, tmx - 1))[None, :]
    tok_tab = lax.shift_right_logical(pairs, 1)
    dst = jnp.where(r[None, :] < nvalid[:, None], (pairs & 1) * plane + tok_tab, spare)
    dst_tab = jnp.concatenate([spare, dst[:-1]], axis=0)
    return tok_tab, dst_tab, te, tend[-1].astype(jnp.int32).reshape(1)


def _ffn_moe_kernel(te_ref, na_ref, tok_ref, dst_ref, h_hbm, w1_ref, w3_ref, w2_ref,
                    o_hbm, tok_s, dst_s, gbuf, h_s, acc_s, out_s, sem_i, sem_g, sem_o,
                    *, tm, n_tok, chunk, nj):
    i = pl.program_id(0)
    j = pl.program_id(1)
    na = na_ref[0]
    live = i <= na
    slot = i & 1
    tmx = chunk * nj
    plane = n_tok + tmx

    def wait_gather(s):
        pltpu.make_async_copy(o_hbm.at[pl.ds(0, tmx)], gbuf.at[s], sem_g.at[s]).wait()

    def wait_scatter():
        pltpu.make_async_copy(out_s, o_hbm.at[pl.ds(0, tmx)], sem_o).wait()

    def load_tables(t):
        cps = [pltpu.make_async_copy(tok_ref.at[t + 1], tok_s, sem_i),
               pltpu.make_async_copy(dst_ref.at[t], dst_s, sem_i)]
        for cp in cps:
            cp.start()
        for cp in cps:
            cp.wait()

    @pl.when(jnp.logical_and(live, j == 0))
    def _():
        @pl.when(i == 0)
        def _():
            cp = pltpu.make_async_copy(tok_ref.at[0], tok_s, sem_i)
            cp.start()
            cp.wait()

            def first(r, carry):
                pltpu.make_async_copy(h_hbm.at[pl.ds(tok_s[r], 1)], gbuf.at[0, pl.ds(r, 1)],
                                      sem_g.at[0]).start()
                return carry
            lax.fori_loop(0, tmx, first, 0)
            out_s[...] = jnp.zeros_like(out_s)
            for c in range(2):
                cz = pltpu.make_async_copy(out_s, o_hbm.at[pl.ds(c * plane + n_tok, tmx)], sem_o)
                cz.start()
                cz.wait()

        wait_gather(slot)
        h_s[...] = gbuf[slot, pl.ds(0, tm), :].astype(BF16)
        acc_s[...] = jnp.zeros_like(acc_s)
        load_tables(i)

    @pl.when(live)
    def _():
        base = j * chunk
        for u in range(chunk):
            r = base + u
            pltpu.make_async_copy(h_hbm.at[pl.ds(tok_s[r], 1)], gbuf.at[1 - slot, pl.ds(r, 1)],
                                  sem_g.at[1 - slot]).start()
            pltpu.make_async_copy(out_s.at[pl.ds(r, 1)], o_hbm.at[pl.ds(dst_s[r], 1)],
                                  sem_o).start()
        h = h_s[...]
        act = _silu(_dot(h, w1_ref[...])) * _dot(h, w3_ref[...])
        acc_s[...] += _dot(act.astype(BF16), w2_ref[...])

    @pl.when(jnp.logical_and(live, j == nj - 1))
    def _():
        wait_scatter()
        out_s[pl.ds(0, tm), :] = acc_s[...]

    @pl.when(jnp.logical_and(i == na + 1, j == 0))
    def _():
        wait_gather(slot)


def _ffn_moe(h, tok_tab, dst_tab, te, n_active, w1, w3, w2, tm):
    n_tok, d = h.shape
    f = w1.shape[2]
    tf = MOE_TF
    assert f % tf == 0
    nj = f // tf
    chunk, tmx, width = _moe_tiling(tm, nj)
    tiles = te.shape[0]
    assert tok_tab.shape == (tiles, width)

    def jj(i, j, na):
        return jnp.where(i <= na[0], j, nj - 1)

    grid_spec = pltpu.PrefetchScalarGridSpec(
        num_scalar_prefetch=2,
        grid=(tiles, nj),
        in_specs=[pl.BlockSpec((tiles, width), lambda i, j, te, na: (0, 0)),
                  pl.BlockSpec((tiles, width), lambda i, j, te, na: (0, 0)),
                  pl.BlockSpec(memory_space=pl.ANY),
                  pl.BlockSpec((None, d, tf), lambda i, j, te, na: (te[i], 0, jj(i, j, na))),
                  pl.BlockSpec((None, d, tf), lambda i, j, te, na: (te[i], 0, jj(i, j, na))),
                  pl.BlockSpec((None, tf, d), lambda i, j, te, na: (te[i], jj(i, j, na), 0))],
        out_specs=pl.BlockSpec(memory_space=pl.ANY),
        scratch_shapes=[pltpu.SMEM((width,), jnp.int32),
                        pltpu.SMEM((width,), jnp.int32),
                        pltpu.VMEM((2, tmx, d), F32),
                        pltpu.VMEM((tm, d), BF16),
                        pltpu.VMEM((tm, d), F32),
                        pltpu.VMEM((tmx, d), F32),
                        pltpu.SemaphoreType.DMA(()),
                        pltpu.SemaphoreType.DMA((2,)),
                        pltpu.SemaphoreType.DMA(())],
    )
    return pl.pallas_call(
        functools.partial(_ffn_moe_kernel, tm=tm, n_tok=n_tok, chunk=chunk, nj=nj),
        grid_spec=grid_spec,
        out_shape=jax.ShapeDtypeStruct((2 * (n_tok + tmx), d), F32),
        compiler_params=_cp("arbitrary", "arbitrary"),
        name="ffn_moe",
    )(te, n_active, tok_tab, dst_tab, h, w1, w3, w2)


def _combine_kernel(*refs, final):
    if final:
        x_ref, gate_ref, r_ref, oa_ref, ob_ref, fg_ref, out_ref = refs
    else:
        x_ref, gate_ref, r_ref, oa_ref, ob_ref, out_ref = refs
    r = r_ref[...]
    y = x_ref[...] + gate_ref[...] * (r[:, 2:3] * oa_ref[...] + r[:, 3:4] * ob_ref[...])
    if final:
        y = _rms(y, fg_ref[...])
    out_ref[...] = y


def _combine(o3, route, xf, gate, seq, final_g=None):
    n, d = xf.shape
    tt = _tile(seq, 512)
    tpb = seq // tt
    final = final_g is not None
    row = lambda i: (i, 0)
    in_specs = [pl.BlockSpec((tt, d), row),
                pl.BlockSpec((None, 1, d), lambda i: (i // tpb, 0, 0)),
                pl.BlockSpec((tt, route.shape[1]), row),
                pl.BlockSpec((None, tt, d), lambda i: (0, i, 0)),
                pl.BlockSpec((None, tt, d), lambda i: (1, i, 0))]
    args = [xf, gate, route, o3, o3]
    if final:
        in_specs.append(pl.BlockSpec((1, d), lambda i: (0, 0)))
        args.append(final_g)
    return pl.pallas_call(
        functools.partial(_combine_kernel, final=final),
        grid=(n // tt,),
        in_specs=in_specs,
        out_specs=pl.BlockSpec((tt, d), row),
        out_shape=jax.ShapeDtypeStruct((n, d), F32),
        compiler_params=_cp("arbitrary"),
        name="moe_combine",
    )(*args)


def _moe_block(xf, sh, sc, g, gate, rw, w1, w3, w2, seq, final_g=None):
    n, d = xf.shape
    tm = 1024 if n >= 32768 else (512 if n >= 4096 else 256)
    h, route = _router(xf, sh, sc, g, rw, seq)
    nj = w1.shape[2] // MOE_TF
    tok_tab, dst_tab, te, n_active = _route_plan(route, tm, nj)
    o = _ffn_moe(h, tok_tab, dst_tab, te, n_active, w1, w3, w2, tm)
    return _combine(o.reshape(2, -1, d), route, xf, gate, seq, final_g)


def _rope_swap(w):
    w4 = w.reshape(w.shape[:-1] + (2, 2, QK_ROPE // 4))
    return jnp.stack([-w4[..., 1, :], w4[..., 0, :]], axis=-2).reshape(w.shape)


def _even_layer_weights(mix_in_w, cq_g, ckv_g, w_uq, w_ukv):
    d = mix_in_w.shape[0]
    s0, s1, s2 = Q_LORA, Q_LORA + KV_LORA, Q_LORA + KV_LORA + QK_ROPE
    wcq, wckv, wkr, wpool = mix_in_w[:, :s0], mix_in_w[:, s0:s1], mix_in_w[:, s1:s2], mix_in_w[:, s2:]
    z = lambda k: jnp.zeros((d, k), F32)
    win = jnp.concatenate([wcq, wckv, wpool,
                           z(QK_NOPE), wkr, z(HEAD_PAD - QK_NOPE - QK_ROPE),
                           z(QK_NOPE), _rope_swap(wkr), z(HEAD_PAD - QK_NOPE - QK_ROPE)], axis=1)
    uq = w_uq.reshape(Q_LORA, MLA_HEADS, QK_NOPE + QK_ROPE)
    zq = lambda k: jnp.zeros((Q_LORA, MLA_HEADS, k), F32)
    qa = jnp.concatenate([uq, zq(HEAD_PAD - QK_NOPE - QK_ROPE)], axis=-1)
    qb = jnp.concatenate([zq(QK_NOPE), _rope_swap(uq[..., QK_NOPE:]),
                          zq(HEAD_PAD - QK_NOPE - QK_ROPE)], axis=-1)
    wq = jnp.concatenate([qa.reshape(Q_LORA, -1), qb.reshape(Q_LORA, -1)], axis=1)
    ukv = w_ukv.reshape(KV_LORA, MLA_HEADS, QK_NOPE + V_DIM)
    wk = jnp.concatenate([ukv[..., :QK_NOPE],
                          jnp.zeros((KV_LORA, MLA_HEADS, HEAD_PAD - QK_NOPE), F32)], axis=-1)
    wv = jnp.concatenate([ukv[..., QK_NOPE:],
                          jnp.zeros((KV_LORA, MLA_HEADS, HEAD_PAD - V_DIM), F32)], axis=-1)
    wkv = jnp.concatenate([wk.reshape(KV_LORA, -1), wv.reshape(KV_LORA, -1)], axis=1)
    return (win.astype(BF16), cq_g.reshape(1, -1), ckv_g.reshape(1, -1), wq.astype(BF16),
            wkv.astype(BF16))


def _rope_tables(seq):
    rows = seq // GRID_W
    row = jnp.repeat(jnp.arange(rows), GRID_W).astype(F32)
    col = jnp.tile(jnp.arange(GRID_W), rows).astype(F32)
    half = QK_ROPE // 2
    inv = 1.0 / (ROPE_BASE ** (jnp.arange(0, half, 2, dtype=F32) / half))
    ang_r = row[:, None] * inv[None, :]
    ang_c = col[:, None] * inv[None, :]
    cos32 = jnp.concatenate([jnp.cos(ang_r)] * 2 + [jnp.cos(ang_c)] * 2, axis=1)
    sin32 = jnp.concatenate([jnp.sin(ang_r)] * 2 + [jnp.sin(ang_c)] * 2, axis=1)
    return _pack_tables(cos32, sin32)


def _pack_tables(cos32, sin32):
    n = cos32.shape[0]
    pad = HEAD_PAD - QK_NOPE - QK_ROPE
    cos_t = jnp.concatenate([jnp.ones((n, QK_NOPE), F32), cos32, jnp.zeros((n, pad), F32)], axis=1)
    sin_t = jnp.concatenate([jnp.zeros((n, QK_NOPE), F32), sin32, jnp.zeros((n, pad), F32)], axis=1)
    q_scale = ATTN_SCALE * math.log2(math.e)
    return jnp.concatenate([cos_t * q_scale, sin_t * q_scale, cos_t, sin_t], axis=1)


def kernel(x, c, ctx, c_ctx, ada_w, ada_b, norm_mix_g, norm_ffn_g, mix_in_w, cq_norm_g, ckv_norm_g,
           w_uq, w_ukv, pool_w, pool_scale, mix_out_w, ffn_w1, ffn_w3, ffn_w2, fnet_out_w,
           router_w, moe_w1, moe_w3, moe_w2, final_g):
    B, L, D = x.shape
    Lc = ctx.shape[1]
    assert ada_w.shape[0] == DEPTH and L % GRID_W == 0
    xf = x.reshape(B * L, D)
    cf = ctx.reshape(B * Lc, D)

    r = ((B + 1 + 7) // 8) * 8
    cvec = jnp.zeros((r, D), F32).at[:B].set(c).at[B].set(c_ctx)
    mod = _adaln(cvec, ada_w, ada_b)

    tab_lat = _rope_tables(L)
    tab_ctx = _pack_tables(jnp.ones((Lc, QK_ROPE), F32), jnp.zeros((Lc, QK_ROPE), F32))
    cos_ch, sin_ch = _dft_tables(D // FNET_GROUPS)
    cs = jnp.asarray(np.concatenate([cos_ch, sin_ch], axis=1), BF16)
    final_g2 = final_g.reshape(1, D)

    for i in range(DEPTH):
        lat = [mod[i, :B, k * D:(k + 1) * D].reshape(B, 1, D) for k in range(6)]
        cx = [jnp.broadcast_to(mod[i, B:B + 1, k * D:(k + 1) * D].reshape(1, 1, D), (B, 1, D))
              for k in range(6)]
        sh1, sc1, g1, sh2, sc2, g2 = lat
        cs1, cc1, cg1, cs2, cc2, cg2 = cx
        gm = norm_mix_g[i].reshape(1, D)
        gf = norm_ffn_g[i].reshape(1, D)
        ctx_needed = any(j % 2 == 0 for j in range(i + 1, DEPTH))
        is_last = i == DEPTH - 1

        if i % 2 == 0:
            e = i // 2
            wts = _even_layer_weights(mix_in_w[e], cq_norm_g[e], ckv_norm_g[e], w_uq[e], w_ukv[e])
            pw = pool_w[e].astype(BF16)
            ps = pool_scale[e].reshape(1, -1)
            wo = mix_out_w[e].astype(BF16)
            wo_p = wo[MLA_HEADS * V_DIM:]
            wo_a = jnp.concatenate(
                [wo[:MLA_HEADS * V_DIM].reshape(MLA_HEADS, V_DIM, D),
                 jnp.zeros((MLA_HEADS, HEAD_PAD - V_DIM, D), BF16)], axis=1).reshape(-1, D)
            w1, w3, w2 = ffn_w1[e].astype(BF16), ffn_w3[e].astype(BF16), ffn_w2[e].astype(BF16)

            q_c, k_c, v_c, p_c = _inproj(cf, cs1, cc1, gm, wts, tab_ctx, Lc)
            q_l, k_l, v_l, p_l = _inproj(xf, sh1, sc1, gm, wts, tab_lat, L)
            attn_l = _attention(q_l, [k_l, k_c], [v_l, v_c], B, L, [L, Lc])
            pool_l = _pool(p_l, pw, ps, B, L)
            xf = _resid_mm(xf, g1, [attn_l, pool_l], [wo_a, wo_p], L)
            xf = _ffn_dense(xf, sh2, sc2, gf, g2, w1, w3, w2, L)
            if ctx_needed:
                attn_c = _attention(q_c, [k_c], [v_c], B, Lc, [Lc])
                pool_c = _pool(p_c, pw, ps, B, Lc)
                cf = _resid_mm(cf, cg1, [attn_c, pool_c], [wo_a, wo_p], Lc)
                cf = _ffn_dense(cf, cs2, cc2, gf, cg2, w1, w3, w2, Lc)
        else:
            o = i // 2
            wf = fnet_out_w[o].astype(BF16)
            rw = jnp.zeros((D, 128), F32).at[:, :N_EXPERTS].set(router_w[o])
            w1, w3, w2 = moe_w1[o].astype(BF16), moe_w3[o].astype(BF16), moe_w2[o].astype(BF16)

            cl, sl = _dft_tables(L)
            yc, ys = _fnet1(xf, sh1, sc1, gm, cs, L)
            xf = _fnet2(jnp.asarray(cl, BF16), jnp.asarray(-sl, BF16), yc, ys, xf, g1, wf, B, L)
            xf = _moe_block(xf, sh2, sc2, gf, g2, rw, w1, w3, w2, L,
                            final_g2 if is_last else None)
            if ctx_needed:
                clc, slc = _dft_tables(Lc)
                yc, ys = _fnet1(cf, cs1, cc1, gm, cs, Lc)
                cf = _fnet2(jnp.asarray(clc, BF16), jnp.asarray(-slc, BF16), yc, ys, cf, cg1, wf,
                            B, Lc)
                cf = _moe_block(cf, cs2, cc2, gf, cg2, rw, w1, w3, w2, Lc)

    if DEPTH % 2 == 1:
        raise NotImplementedError("final norm is fused into the last (odd) layer")
    return xf.reshape(B, L, D)
```
